```python
import math
import jax, jax.numpy as jnp
from jax import lax
import numpy as np

D_MODEL = 1024
BATCH = 32
SEQ = 2048
DEPTH = 4

N_MEM = 256
HEAD_DIM = 64
N_DIL_HEADS = 12
DIL_WIDTH = N_DIL_HEADS * HEAD_DIM
DIL_PATTERNS = ((128, 1), (512, 4), (2048, 16))
BLOCK = 128
N_SGU_GROUPS = 12
SGU_GROUP_DIM = 64
SGU_WIDTH = N_SGU_GROUPS * SGU_GROUP_DIM
CHUNK = 128
N_MEM_HEADS = 4
MEM_WIDTH = N_MEM_HEADS * HEAD_DIM
MIX_WIDTH = DIL_WIDTH + MEM_WIDTH
D_FF = ((-(-8 * D_MODEL // 3) + 255) // 256) * 256
DN_ALPHA = (2 * DEPTH) ** 0.25
DN_BETA = (8 * DEPTH) ** -0.25
N_A = (DEPTH + 1) // 2
N_B = DEPTH // 2
IN_A = 3 * DIL_WIDTH + MEM_WIDTH
IN_B = 2 * SGU_WIDTH + MEM_WIDTH
LN_EPS = 1e-5

kernel_name = "hybrid_dilated_gmlp_memory_deepnorm"


def layer_norm(x, g, b):
    xf = x.astype(jnp.float32)
    mu = xf.mean(-1, keepdims=True)
    var = jnp.square(xf - mu).mean(-1, keepdims=True)
    y = (xf - mu) * lax.rsqrt(var + LN_EPS)
    return (y * g.astype(jnp.float32) + b.astype(jnp.float32)).astype(x.dtype)


def alibi_slopes(n):
    return jnp.exp2(-8.0 * (jnp.arange(n, dtype=jnp.float32) + 1.0) / n)


def band_pattern(q, k, v, slopes, window, dil):
    B, S, H, Dh = q.shape
    steps_max = window // dil
    L = S // dil
    n_blk = -(-L // BLOCK)
    Lp = n_blk * BLOCK
    def sub(t):
        return t.reshape(B, L, dil, H, Dh)
    qs = jnp.pad(sub(q), ((0, 0), (0, Lp - L), (0, 0), (0, 0), (0, 0)))
    qs = qs.reshape(B, n_blk, BLOCK, dil, H, Dh)
    def banded_keys(t):
        tp = jnp.pad(sub(t), ((0, 0), (BLOCK, Lp - L), (0, 0), (0, 0), (0, 0)))
        prev = tp[:, :Lp].reshape(B, n_blk, BLOCK, dil, H, Dh)
        cur = tp[:, BLOCK:].reshape(B, n_blk, BLOCK, dil, H, Dh)
        return jnp.concatenate([prev, cur], axis=2)
    kb = banded_keys(k)
    vb = banded_keys(v)
    s = jnp.einsum('bnqrhd,bnkrhd->bnrhqk', qs, kb).astype(jnp.float32) * (Dh ** -0.5)
    qi = jnp.arange(BLOCK)[:, None]
    ki = jnp.arange(2 * BLOCK)[None, :]
    steps = qi + BLOCK - ki
    key_idx = (jnp.arange(n_blk) * BLOCK)[:, None, None] - BLOCK + ki[None]
    valid = (steps >= 0) & (steps <= steps_max) & (key_idx >= 0)
    bias = -slopes[:, None, None] * (steps * dil).astype(jnp.float32)[None]
    s = s + bias[None, None, None]
    s = jnp.where(valid[None, :, None, None], s, -jnp.inf)
    m = s.max(-1, keepdims=True)
    p = jnp.exp(s - m)
    den = p.sum(-1)
    lse = m[..., 0] + jnp.log(den)
    o = jnp.einsum('bnrhqk,bnkrhd->bnqrhd', p.astype(v.dtype), vb)
    o = o / jnp.moveaxis(den, -1, 2)[..., None].astype(o.dtype)
    o = o.reshape(B, Lp, dil, H, Dh)[:, :L].reshape(B, S, H, Dh)
    lse = jnp.moveaxis(lse, -1, 2).reshape(B, Lp, dil, H)[:, :L].reshape(B, S, H)
    return o, lse


def dilated_attention(q, k, v):
    slopes = alibi_slopes(q.shape[2])
    outs, lses = [], []
    for window, dil in DIL_PATTERNS:
        o, lse = band_pattern(q, k, v, slopes, window, dil)
        outs.append(o)
        lses.append(lse)
    w = jax.nn.softmax(jnp.stack(lses, 0), axis=0).astype(q.dtype)
    return jnp.einsum('pbsh,pbshd->bshd', w, jnp.stack(outs, 0))


def causal_chunk_sgu(u, v, ln_g, ln_b, w_s, b_s):
    B, S, _ = v.shape
    v = layer_norm(v, ln_g, ln_b)
    vc = v.reshape(B, S // CHUNK, CHUNK, N_SGU_GROUPS, SGU_GROUP_DIM)
    ws = w_s * jnp.tril(jnp.ones((CHUNK, CHUNK), w_s.dtype))
    mixed = jnp.einsum('gts,bnsgc->bntgc', ws, vc) + b_s.T[None, None, :, :, None]
    return u * mixed.reshape(B, S, SGU_WIDTH)


def memory_attention(qm, mk, mv):
    s = jnp.einsum('bshd,bmhd->bhsm', qm, mk).astype(jnp.float32) * (HEAD_DIM ** -0.5)
    p = jax.nn.softmax(s, axis=-1).astype(mv.dtype)
    return jnp.einsum('bhsm,bmhd->bshd', p, mv)


def setup_inputs(seed: int = 0) -> dict:
    key = jax.random.key(seed)
    ks = jax.random.split(key, 18)
    f32 = jnp.float32
    D = D_MODEL
    def nrm(k, shape, scale):
        return jax.random.normal(k, shape, f32) * scale
    return {
        "x": nrm(ks[0], (BATCH, SEQ, D), 1.0),
        "mem": nrm(ks[1], (BATCH, N_MEM, D), 1.0),
        "a_w_in": nrm(ks[2], (N_A, D, IN_A), D ** -0.5),
        "b_w_in": nrm(ks[3], (N_B, D, IN_B), D ** -0.5),
        "sgu_ln_g": 1.0 + nrm(ks[4], (N_B, SGU_WIDTH), 0.02),
        "sgu_ln_b": nrm(ks[5], (N_B, SGU_WIDTH), 0.02),
        "sgu_w_s": nrm(ks[6], (N_B, N_SGU_GROUPS, CHUNK, CHUNK), CHUNK ** -0.5),
        "sgu_b_s": 1.0 + nrm(ks[7], (N_B, N_SGU_GROUPS, CHUNK), 0.02),
        "w_mem_kv": nrm(ks[8], (DEPTH, D, 2 * MEM_WIDTH), D ** -0.5),
        "w_out": nrm(ks[9], (DEPTH, MIX_WIDTH, D), DN_BETA * MIX_WIDTH ** -0.5),
        "ln_mix_g": 1.0 + nrm(ks[10], (DEPTH, D), 0.02),
        "ln_mix_b": nrm(ks[11], (DEPTH, D), 0.02),
        "w_gate": nrm(ks[12], (DEPTH, D, D_FF), D ** -0.5),
        "w_up": nrm(ks[13], (DEPTH, D, D_FF), D ** -0.5),
        "w_down": nrm(ks[14], (DEPTH, D_FF, D), DN_BETA * D_FF ** -0.5),
        "ln_ffn_g": 1.0 + nrm(ks[15], (DEPTH, D), 0.02),
        "ln_ffn_b": nrm(ks[16], (DEPTH, D), 0.02),
    }


def reference(x, mem, a_w_in, b_w_in, sgu_ln_g, sgu_ln_b, sgu_w_s, sgu_b_s, w_mem_kv, w_out,
              ln_mix_g, ln_mix_b, w_gate, w_up, w_down, ln_ffn_g, ln_ffn_b):
    B, S, _ = x.shape
    for i in range(DEPTH):
        j = i // 2
        mk, mv = jnp.split(mem @ w_mem_kv[i], 2, axis=-1)
        mk = mk.reshape(B, N_MEM, N_MEM_HEADS, HEAD_DIM)
        mv = mv.reshape(B, N_MEM, N_MEM_HEADS, HEAD_DIM)
        if i % 2 == 0:
            h = x @ a_w_in[j]
            q, k, v, qm = jnp.split(h, [DIL_WIDTH, 2 * DIL_WIDTH, 3 * DIL_WIDTH], axis=-1)
            hs = (B, S, N_DIL_HEADS, HEAD_DIM)
            mix = dilated_attention(q.reshape(hs), k.reshape(hs), v.reshape(hs)).reshape(B, S, DIL_WIDTH)
        else:
            h = x @ b_w_in[j]
            u, v, qm = jnp.split(h, [SGU_WIDTH, 2 * SGU_WIDTH], axis=-1)
            mix = causal_chunk_sgu(jax.nn.gelu(u), jax.nn.gelu(v), sgu_ln_g[j], sgu_ln_b[j],
                                   sgu_w_s[j], sgu_b_s[j])
        mo = memory_attention(qm.reshape(B, S, N_MEM_HEADS, HEAD_DIM), mk, mv).reshape(B, S, MEM_WIDTH)
        y = jnp.concatenate([mix, mo], axis=-1) @ w_out[i]
        x = layer_norm(DN_ALPHA * x + y, ln_mix_g[i], ln_mix_b[i])
        f = (jax.nn.silu(x @ w_gate[i]) * (x @ w_up[i])) @ w_down[i]
        x = layer_norm(DN_ALPHA * x + f, ln_ffn_g[i], ln_ffn_b[i])
    return x
```

```python
import functools

import jax
import jax.numpy as jnp
from jax import lax
from jax.experimental import pallas as pl
from jax.experimental.pallas import tpu as pltpu

F32 = jnp.float32
BF16 = jnp.bfloat16

D_MODEL = 1024
N_MEM = 256
HEAD_DIM = 64
N_DIL_HEADS = 12
DIL_WIDTH = N_DIL_HEADS * HEAD_DIM
DIL_PATTERNS = ((128, 1), (512, 4), (2048, 16))
BLOCK = 128
N_SGU_GROUPS = 12
SGU_WIDTH = N_SGU_GROUPS * HEAD_DIM
CHUNK = 128
MEM_WIDTH = 4 * HEAD_DIM
DEPTH = 4
DN_ALPHA = (2 * DEPTH) ** 0.25
LN_EPS = 1e-5

LANES = 128
HEAD_PAIRS = DIL_WIDTH // LANES
Q_SCALE = HEAD_DIM ** -0.5
VMEM_LIMIT = 56 * 1024 * 1024


def _params(*sem):
    return pltpu.CompilerParams(dimension_semantics=sem, vmem_limit_bytes=VMEM_LIMIT)


def _layer_norm(z, g, b):
    mu = jnp.mean(z, axis=-1, keepdims=True)
    zc = z - mu
    var = jnp.mean(zc * zc, axis=-1, keepdims=True)
    return zc * lax.rsqrt(var + LN_EPS) * g + b


def _aligned(idx, multiple):
    return idx if isinstance(idx, int) else pl.multiple_of(idx, multiple)


def _dot(a, b):
    return jnp.dot(a, b, preferred_element_type=F32)


def _dot_nt(a, b):
    return lax.dot_general(a, b, (((1,), (1,)), ((), ())), preferred_element_type=F32)


def _proj_kernel(splits, x_ref, w_ref, *out_refs):
    acc = _dot(x_ref[...].astype(BF16), w_ref[...])
    for (start, width), o_ref in zip(splits, out_refs):
        o_ref[...] = acc[:, start:start + width].astype(o_ref.dtype)


def _project(x, w, splits, dtypes, tm):
    m, k = x.shape
    n = w.shape[1]
    return pl.pallas_call(
        functools.partial(_proj_kernel, splits),
        grid=(m // tm,),
        in_specs=[pl.BlockSpec((tm, k), lambda i: (i, 0)),
                  pl.BlockSpec((k, n), lambda i: (0, 0))],
        out_specs=[pl.BlockSpec((tm, width), lambda i: (i, 0)) for _, width in splits],
        out_shape=[jax.ShapeDtypeStruct((m, width), dt) for (_, width), dt in zip(splits, dtypes)],
        compiler_params=_params("parallel"),
        name="in_proj",
    )(x, w)


def _dil_attn_kernel(q_ref, k_ref, v_ref, o_ref, qf, kf, vf, qp, kp, vp, acc_o, acc_l):
    seq = q_ref.shape[0]
    hp = pl.program_id(1)
    lane = lax.broadcasted_iota(jnp.int32, (BLOCK, LANES), 1)
    lo = lane < HEAD_DIM

    qf[...] = q_ref[...].astype(F32) * Q_SCALE
    kf[...] = k_ref[...].astype(F32)
    vf[...] = v_ref[...].astype(F32)

    qi = lax.broadcasted_iota(jnp.int32, (BLOCK, 2 * BLOCK), 0)
    kc = lax.broadcasted_iota(jnp.int32, (BLOCK, 2 * BLOCK), 1)
    steps = qi + BLOCK - kc
    head = jnp.full((BLOCK, 2 * BLOCK), 2 * hp, jnp.int32).astype(F32)
    slopes = [jnp.exp2(-8.0 * (head + (hh + 1.0)) / N_DIL_HEADS) for hh in range(2)]

    for p_idx, (window, dil) in enumerate(DIL_PATTERNS):
        sub_len = seq // dil
        n_blk = sub_len // BLOCK
        valid = (steps >= 0) & (steps <= window // dil)
        dist = (steps * dil).astype(F32)
        bias = [jnp.where(valid, -sl * dist, -jnp.inf) for sl in slopes]

        for r in range(dil):
            rows = pl.ds(r, sub_len, stride=dil) if dil > 1 else pl.ds(0, sub_len)
            dst = pl.ds(r * sub_len, sub_len)
            qp[dst, :] = qf[rows, :].astype(BF16)
            kp[dst, :] = kf[rows, :].astype(BF16)
            vp[dst, :] = vf[rows, :].astype(BF16)

        def unit(row0, out_start, has_prev, p_idx=p_idx, dil=dil, bias=bias):
            row0 = _aligned(row0, BLOCK)
            qb = qp[pl.ds(row0, BLOCK), :]
            if has_prev:
                kv_rows = pl.ds(_aligned(row0 - BLOCK, BLOCK), 2 * BLOCK)
            else:
                kv_rows = pl.ds(row0, BLOCK)
            kb = kp[kv_rows, :]
            vb = vp[kv_rows, :]
            outs, lses = [], []
            for hh in range(2):
                sel = lo if hh == 0 else jnp.logical_not(lo)
                qh = jnp.where(sel, qb, jnp.zeros_like(qb))
                c = bias[hh] if has_prev else bias[hh][:, BLOCK:]
                s = _dot_nt(qh, kb) + c
                m = jnp.max(s, axis=-1, keepdims=True)
                e = jnp.exp(s - m)
                den = jnp.sum(e, axis=-1, keepdims=True)
                pv = _dot(e.astype(BF16), vb)
                outs.append(pv / den)
                lses.append(m + jnp.log(den))
            o_tile = jnp.where(lo, outs[0], outs[1])
            l_tile = jnp.where(lo, lses[0], lses[1])
            if dil > 1:
                dst = pl.ds(out_start, BLOCK, stride=dil)
            else:
                dst = pl.ds(_aligned(out_start, BLOCK), BLOCK)
            acc_o[p_idx, dst, :] = o_tile
            acc_l[p_idx, dst, :] = l_tile

        def sub_loop(r, carry, unit=unit, sub_len=sub_len, n_blk=n_blk, dil=dil):
            base = r * sub_len
            unit(base, r, False)
            if n_blk > 1:
                def blk_loop(n, c):
                    unit(base + n * BLOCK, n * BLOCK * dil + r, True)
                    return c
                lax.fori_loop(1, n_blk, blk_loop, 0)
            return carry

        if dil == 1:
            sub_loop(0, 0)
        else:
            lax.fori_loop(0, dil, sub_loop, 0)

    merge_rows = 256

    def merge(c, carry):
        rows = pl.ds(pl.multiple_of(c * merge_rows, merge_rows), merge_rows)
        ls = [acc_l[p, rows, :] for p in range(len(DIL_PATTERNS))]
        m = jnp.maximum(jnp.maximum(ls[0], ls[1]), ls[2])
        es = [jnp.exp(l - m) for l in ls]
        num = es[0] * acc_o[0, rows, :] + es[1] * acc_o[1, rows, :] + es[2] * acc_o[2, rows, :]
        o_ref[rows, :] = (num / (es[0] + es[1] + es[2])).astype(o_ref.dtype)
        return carry

    lax.fori_loop(0, seq // merge_rows, merge, 0)


def _dilated_attention(h):
    b, s, _ = h.shape
    blk = lambda off: pl.BlockSpec((None, s, LANES), lambda i, j: (i, 0, j + off))
    return pl.pallas_call(
        _dil_attn_kernel,
        grid=(b, HEAD_PAIRS),
        in_specs=[blk(0), blk(HEAD_PAIRS), blk(2 * HEAD_PAIRS)],
        out_specs=pl.BlockSpec((None, s, LANES), lambda i, j: (i, 0, j)),
        out_shape=jax.ShapeDtypeStruct((b, s, DIL_WIDTH), BF16),
        scratch_shapes=[pltpu.VMEM((s, LANES), F32)] * 3 + [pltpu.VMEM((s, LANES), BF16)] * 3
        + [pltpu.VMEM((len(DIL_PATTERNS), s, LANES), F32)] * 2,
        compiler_params=_params("parallel", "parallel"),
        name="dilated_attention",
    )(h, h, h)


def _mem_attn_kernel(q_ref, mk_ref, mv_ref, o_ref):
    seq = q_ref.shape[0]
    rows_per_step = 256
    lane = lax.broadcasted_iota(jnp.int32, (rows_per_step, LANES), 1)
    lo = lane < HEAD_DIM
    mk = mk_ref[...]
    mv = mv_ref[...]

    def body(c, carry):
        rows = pl.ds(pl.multiple_of(c * rows_per_step, rows_per_step), rows_per_step)
        qb = (q_ref[rows, :].astype(F32) * Q_SCALE).astype(BF16)
        outs = []
        for hh in range(2):
            sel = lo if hh == 0 else jnp.logical_not(lo)
            qh = jnp.where(sel, qb, jnp.zeros_like(qb))
            s = _dot_nt(qh, mk)
            m = jnp.max(s, axis=-1, keepdims=True)
            e = jnp.exp(s - m)
            den = jnp.sum(e, axis=-1, keepdims=True)
            outs.append(_dot(e.astype(BF16), mv) / den)
        o_ref[rows, :] = jnp.where(lo, outs[0], outs[1]).astype(o_ref.dtype)
        return carry

    lax.fori_loop(0, seq // rows_per_step, body, 0)


def _memory_attention(qm_src, qm_col0, mkv):
    b, s, _ = qm_src.shape
    pairs = MEM_WIDTH // LANES
    return pl.pallas_call(
        _mem_attn_kernel,
        grid=(b, pairs),
        in_specs=[pl.BlockSpec((None, s, LANES), lambda i, j: (i, 0, j + qm_col0)),
                  pl.BlockSpec((None, N_MEM, LANES), lambda i, j: (i, 0, j)),
                  pl.BlockSpec((None, N_MEM, LANES), lambda i, j: (i, 0, j + pairs))],
        out_specs=pl.BlockSpec((None, s, LANES), lambda i, j: (i, 0, j)),
        out_shape=jax.ShapeDtypeStruct((b, s, MEM_WIDTH), BF16),
        compiler_params=_params("parallel", "parallel"),
        name="memory_attention",
    )(qm_src, mkv, mkv)


def _sgu_kernel(u_ref, v_ref, g_ref, b_ref, ws_ref, bias_ref, o_ref, vn_ref):
    tm = u_ref.shape[0]
    vn_ref[...] = _layer_norm(jax.nn.gelu(v_ref[...]), g_ref[...], b_ref[...]).astype(BF16)
    row = lax.broadcasted_iota(jnp.int32, (CHUNK, CHUNK), 0)
    col = lax.broadcasted_iota(jnp.int32, (CHUNK, CHUNK), 1)
    causal = row >= col
    lo = lax.broadcasted_iota(jnp.int32, (CHUNK, LANES), 1) < HEAD_DIM
    for gp in range(N_SGU_GROUPS // 2):
        cols = slice(gp * LANES, (gp + 1) * LANES)
        w0 = jnp.where(causal, ws_ref[2 * gp], 0.0).astype(BF16)
        w1 = jnp.where(causal, ws_ref[2 * gp + 1], 0.0).astype(BF16)
        bias = bias_ref[:, cols]
        for c in range(tm // CHUNK):
            rows = slice(c * CHUNK, (c + 1) * CHUNK)
            vc = vn_ref[rows, cols]
            mixed = jnp.where(lo, _dot(w0, vc), _dot(w1, vc)) + bias
            o_ref[rows, cols] = (jax.nn.gelu(u_ref[rows, cols]) * mixed).astype(o_ref.dtype)


def _sgu(uv, ln_g, ln_b, w_s, bias_full, tm):
    m = uv.shape[0]
    return pl.pallas_call(
        _sgu_kernel,
        grid=(m // tm,),
        in_specs=[pl.BlockSpec((tm, SGU_WIDTH), lambda i: (i, 0)),
                  pl.BlockSpec((tm, SGU_WIDTH), lambda i: (i, 1)),
                  pl.BlockSpec((1, SGU_WIDTH), lambda i: (0, 0)),
                  pl.BlockSpec((1, SGU_WIDTH), lambda i: (0, 0)),
                  pl.BlockSpec((N_SGU_GROUPS, CHUNK, CHUNK), lambda i: (0, 0, 0)),
                  pl.BlockSpec((CHUNK, SGU_WIDTH), lambda i: (0, 0))],
        out_specs=pl.BlockSpec((tm, SGU_WIDTH), lambda i: (i, 0)),
        out_shape=jax.ShapeDtypeStruct((m, SGU_WIDTH), BF16),
        scratch_shapes=[pltpu.VMEM((tm, SGU_WIDTH), BF16)],
        compiler_params=_params("parallel"),
        name="sgu",
    )(uv, uv, ln_g, ln_b, w_s, bias_full)


def _out_proj_kernel(mix_ref, mo_ref, x_ref, wa_ref, wb_ref, g_ref, b_ref, o_ref):
    y = _dot(mix_ref[...], wa_ref[...]) + _dot(mo_ref[...], wb_ref[...])
    o_ref[...] = _layer_norm(DN_ALPHA * x_ref[...] + y, g_ref[...], b_ref[...])


def _out_proj(mix, mo, x, w_out, g, b, tm):
    m, d = x.shape
    wm = mix.shape[1]
    return pl.pallas_call(
        _out_proj_kernel,
        grid=(m // tm,),
        in_specs=[pl.BlockSpec((tm, wm), lambda i: (i, 0)),
                  pl.BlockSpec((tm, MEM_WIDTH), lambda i: (i, 0)),
                  pl.BlockSpec((tm, d), lambda i: (i, 0)),
                  pl.BlockSpec((wm, d), lambda i: (0, 0)),
                  pl.BlockSpec((MEM_WIDTH, d), lambda i: (wm // MEM_WIDTH, 0)),
                  pl.BlockSpec((1, d), lambda i: (0, 0)),
                  pl.BlockSpec((1, d), lambda i: (0, 0))],
        out_specs=pl.BlockSpec((tm, d), lambda i: (i, 0)),
        out_shape=jax.ShapeDtypeStruct((m, d), F32),
        compiler_params=_params("parallel"),
        name="out_proj_ln",
    )(mix, mo, x, w_out, w_out, g, b)


def _ffn_kernel(x_ref, wg_ref, wu_ref, wd_ref, g_ref, b_ref, o_ref, xb_ref, acc_ref):
    j = pl.program_id(1)

    @pl.when(j == 0)
    def _():
        xb_ref[...] = x_ref[...].astype(BF16)
        acc_ref[...] = jnp.zeros_like(acc_ref)

    xb = xb_ref[...]
    hid = jax.nn.silu(_dot(xb, wg_ref[...])) * _dot(xb, wu_ref[...])
    acc_ref[...] += _dot(hid.astype(BF16), wd_ref[...])

    @pl.when(j == pl.num_programs(1) - 1)
    def _():
        o_ref[...] = _layer_norm(DN_ALPHA * x_ref[...] + acc_ref[...], g_ref[...], b_ref[...])


def _ffn(x, wg, wu, wd, g, b, tm, tf):
    m, d = x.shape
    f = wg.shape[1]
    return pl.pallas_call(
        _ffn_kernel,
        grid=(m // tm, f // tf),
        in_specs=[pl.BlockSpec((tm, d), lambda i, j: (i, 0)),
                  pl.BlockSpec((d, tf), lambda i, j: (0, j)),
                  pl.BlockSpec((d, tf), lambda i, j: (0, j)),
                  pl.BlockSpec((tf, d), lambda i, j: (j, 0)),
                  pl.BlockSpec((1, d), lambda i, j: (0, 0)),
                  pl.BlockSpec((1, d), lambda i, j: (0, 0))],
        out_specs=pl.BlockSpec((tm, d), lambda i, j: (i, 0)),
        out_shape=jax.ShapeDtypeStruct((m, d), F32),
        scratch_shapes=[pltpu.VMEM((tm, d), BF16), pltpu.VMEM((tm, d), F32)],
        compiler_params=_params("parallel", "arbitrary"),
        name="ffn_ln",
    )(x, wg, wu, wd, g, b)


def kernel(x, mem, a_w_in, b_w_in, sgu_ln_g, sgu_ln_b, sgu_w_s, sgu_b_s, w_mem_kv, w_out,
           ln_mix_g, ln_mix_b, w_gate, w_up, w_down, ln_ffn_g, ln_ffn_b):
    bsz, seq, d = x.shape
    m = bsz * seq
    xf = x.reshape(m, d)
    memf = mem.reshape(bsz * N_MEM, d)
    row = lambda a: a.reshape(1, -1)
    for i in range(DEPTH):
        j = i // 2
        (mkv,) = _project(memf, w_mem_kv[i].astype(BF16), [(0, 2 * MEM_WIDTH)], [BF16], tm=512)
        mkv = mkv.reshape(bsz, N_MEM, 2 * MEM_WIDTH)
        if i % 2 == 0:
            (h,) = _project(xf, a_w_in[j].astype(BF16), [(0, 3 * DIL_WIDTH + MEM_WIDTH)], [BF16], tm=512)
            h = h.reshape(bsz, seq, -1)
            mix = _dilated_attention(h).reshape(m, DIL_WIDTH)
            mo = _memory_attention(h, 3 * HEAD_PAIRS, mkv)
        else:
            uv, qm = _project(xf, b_w_in[j].astype(BF16),
                              [(0, 2 * SGU_WIDTH), (2 * SGU_WIDTH, MEM_WIDTH)], [F32, BF16], tm=512)
            bias_full = jnp.repeat(sgu_b_s[j].T, HEAD_DIM, axis=1)
            mix = _sgu(uv, row(sgu_ln_g[j]), row(sgu_ln_b[j]), sgu_w_s[j], bias_full, tm=512)
            mo = _memory_attention(qm.reshape(bsz, seq, MEM_WIDTH), 0, mkv)
        xf = _out_proj(mix, mo.reshape(m, MEM_WIDTH), xf, w_out[i].astype(BF16),
                       row(ln_mix_g[i]), row(ln_mix_b[i]), tm=512)
        xf = _ffn(xf, w_gate[i].astype(BF16), w_up[i].astype(BF16), w_down[i].astype(BF16),
                  row(ln_ffn_g[i]), row(ln_ffn_b[i]), tm=1024, tf=256)
    return xf.reshape(bsz, seq, d)
```

```python
import functools

import jax
import jax.numpy as jnp
from jax import lax
from jax.experimental import pallas as pl
from jax.experimental.pallas import tpu as pltpu

F32 = jnp.float32
BF16 = jnp.bfloat16

D_MODEL = 1024
N_MEM = 256
HEAD_DIM = 64
N_DIL_HEADS = 12
DIL_WIDTH = N_DIL_HEADS * HEAD_DIM
DIL_PATTERNS = ((128, 1), (512, 4), (2048, 16))
BLOCK = 128
N_SGU_GROUPS = 12
SGU_WIDTH = N_SGU_GROUPS * HEAD_DIM
CHUNK = 128
MEM_WIDTH = 4 * HEAD_DIM
DEPTH = 4
DN_ALPHA = (2 * DEPTH) ** 0.25
LN_EPS = 1e-5

LANES = 128
HEAD_PAIRS = DIL_WIDTH // LANES
Q_SCALE = HEAD_DIM ** -0.5
VMEM_LIMIT = 56 * 1024 * 1024
FF_CHUNK = 256


def _params(*sem):
    return pltpu.CompilerParams(dimension_semantics=sem, vmem_limit_bytes=VMEM_LIMIT)


def _layer_norm(z, g, b):
    mu = jnp.mean(z, axis=-1, keepdims=True)
    zc = z - mu
    var = jnp.mean(zc * zc, axis=-1, keepdims=True)
    return zc * lax.rsqrt(var + LN_EPS) * g + b


def _aligned(idx, multiple):
    return idx if isinstance(idx, int) else pl.multiple_of(idx, multiple)


def _dot(a, b):
    return jnp.dot(a, b, preferred_element_type=F32)


def _dot_nt(a, b):
    return lax.dot_general(a, b, (((1,), (1,)), ((), ())), preferred_element_type=F32)


def _resident(shape):
    return pl.BlockSpec(shape, lambda *_: (0,) * len(shape), pipeline_mode=pl.Buffered(1))


def _softmax_pv(q2, kb, v1, bias):
    s = _dot_nt(q2, kb)
    if bias is not None:
        s = s + bias
    m = jnp.max(s, axis=-1, keepdims=True)
    pv = _dot(jnp.exp(s - m).astype(BF16), v1)
    return pv[:, :LANES], m, pv[:, LANES:]


def _proj_kernel(splits, x_ref, w_ref, *out_refs):
    acc = _dot(x_ref[...].astype(BF16), w_ref[...])
    for (start, width), o_ref in zip(splits, out_refs):
        o_ref[...] = acc[:, start:start + width].astype(o_ref.dtype)


def _project(x, w, splits, dtypes, tm):
    m, k = x.shape
    n = w.shape[1]
    return pl.pallas_call(
        functools.partial(_proj_kernel, splits),
        grid=(m // tm,),
        in_specs=[pl.BlockSpec((tm, k), lambda i: (i, 0)), _resident((k, n))],
        out_specs=[pl.BlockSpec((tm, width), lambda i: (i, 0)) for _, width in splits],
        out_shape=[jax.ShapeDtypeStruct((m, width), dt) for (_, width), dt in zip(splits, dtypes)],
        compiler_params=_params("parallel"),
        name="in_proj",
    )(x, w)


N_PAT = len(DIL_PATTERNS)
Q, K, V = range(3)
PV, ROW_MAX, ROW_SUM = range(3)
UNITS_PER_ROUND = 16


def _dil_attn_kernel(q_ref, k_ref, v_ref, o_ref, lay, q2, kp, vp, bias_ref, acc, tmp):
    seq = q_ref.shape[0]
    hp = pl.program_id(1)
    lo = lax.broadcasted_iota(jnp.int32, (BLOCK, LANES), 1) < HEAD_DIM

    lay[Q] = q_ref[...].astype(F32) * Q_SCALE
    lay[K] = k_ref[...].astype(F32)
    lay[V] = v_ref[...].astype(F32)

    row = lax.broadcasted_iota(jnp.int32, (2 * BLOCK, 2 * BLOCK), 0)
    kc = lax.broadcasted_iota(jnp.int32, (2 * BLOCK, 2 * BLOCK), 1)
    steps = (row & (BLOCK - 1)) + BLOCK - kc
    head = (jnp.full((2 * BLOCK, 2 * BLOCK), 2 * hp, jnp.int32) + (row >> 7)).astype(F32)
    slope = jnp.exp2(-8.0 * (head + 1.0) / N_DIL_HEADS)

    prev_dil = 1
    for p_idx, (window, dil) in enumerate(DIL_PATTERNS):
        sub_len = seq // dil
        n_blk = sub_len // BLOCK
        ratio = dil // prev_dil
        prev_len = seq // prev_dil
        src, dst = 3 * ((p_idx + 1) % 2), 3 * (p_idx % 2)
        assert prev_dil * ratio == dil and (prev_dil == 1 or p_idx == 2)

        valid = (steps >= 0) & (steps <= window // dil)
        bias_ref[...] = jnp.where(valid, -slope * (steps * dil).astype(F32), -jnp.inf)

        if ratio == 1:
            lo_all = lax.broadcasted_iota(jnp.int32, (seq, LANES), 1) < HEAD_DIM
            q2[0] = jnp.where(lo_all, lay[dst + Q], 0.0).astype(BF16)
            q2[1] = jnp.where(lo_all, 0.0, lay[dst + Q]).astype(BF16)
            k_src, v_src = k_ref, v_ref
        else:
            lo_sub = lax.broadcasted_iota(jnp.int32, (sub_len, LANES), 1) < HEAD_DIM
            for r_prev in range(prev_dil):
                for c in range(ratio):
                    rows = pl.ds(r_prev * prev_len + c, sub_len, stride=ratio)
                    out_rows = pl.ds((r_prev + prev_dil * c) * sub_len, sub_len)
                    qr, kr, vr = lay[src + Q, rows, :], lay[src + K, rows, :], lay[src + V, rows, :]
                    if p_idx + 1 < N_PAT:
                        lay[dst + Q, out_rows, :] = qr
                        lay[dst + K, out_rows, :] = kr
                        lay[dst + V, out_rows, :] = vr
                    q2[0, out_rows, :] = jnp.where(lo_sub, qr, 0.0).astype(BF16)
                    q2[1, out_rows, :] = jnp.where(lo_sub, 0.0, qr).astype(BF16)
                    kp[out_rows, :] = kr.astype(BF16)
                    vp[out_rows, :] = vr.astype(BF16)
            k_src, v_src = kp, vp

        def store_for(r, n, p_idx=p_idx, dil=dil, ratio=ratio, prev_dil=prev_dil, prev_len=prev_len):
            if dil == 1:
                ref, base, rows = acc, 3 * p_idx, pl.ds(_aligned(n * BLOCK, BLOCK), BLOCK)
            elif prev_dil == 1:
                ref, base, rows = acc, 3 * p_idx, pl.ds(n * BLOCK * dil + r, BLOCK, stride=dil)
            else:
                start = lax.rem(r, prev_dil) * prev_len + n * BLOCK * ratio + lax.div(r, prev_dil)
                ref, base, rows = tmp, 0, pl.ds(start, BLOCK, stride=ratio)

            def store(tiles):
                for slot, tile in enumerate(tiles):
                    ref[base + slot, rows, :] = tile
            return store

        def run_round(blocks, k_src=k_src, v_src=v_src, store_for=store_for, sub_len=sub_len):
            state = [dict() for _ in blocks]

            def scores(b):
                r, n = blocks[b]
                row0 = _aligned(r * sub_len + n * BLOCK, BLOCK)
                qq = q2[:, pl.ds(row0, BLOCK), :].reshape(2 * BLOCK, LANES)
                if n > 0:
                    rows, bias = pl.ds(_aligned(row0 - BLOCK, BLOCK), 2 * BLOCK), bias_ref[...]
                else:
                    rows, bias = pl.ds(row0, BLOCK), bias_ref[:, BLOCK:]
                state[b].update(rows=rows, s=_dot_nt(qq, k_src[rows, :]) + bias)

            def row_max(b):
                state[b]["m"] = jnp.max(state[b]["s"], axis=-1, keepdims=True)

            def weighted_values(b):
                st = state[b]
                p = jnp.exp(st.pop("s") - st["m"]).astype(BF16)
                vb = v_src[st["rows"], :]
                st["pv"] = _dot(p, jnp.concatenate([vb, jnp.ones_like(vb)], axis=1))

            def store(b):
                st = state[b]
                pv = st.pop("pv")
                stats = (pv[:, :LANES], st.pop("m"), pv[:, LANES:])
                store_for(*blocks[b])([jnp.where(lo, t[:BLOCK], t[BLOCK:]) for t in stats])

            stages = ((scores, 0), (row_max, 2), (weighted_values, 3), (store, 5))
            for t in range(len(blocks) + stages[-1][1]):
                for stage, lag in stages:
                    if 0 <= t - lag < len(blocks):
                        stage(t - lag)

        subs_per_round = max(1, UNITS_PER_ROUND // n_blk)
        blks_per_round = min(n_blk, UNITS_PER_ROUND)
        for n0 in range(0, n_blk, blks_per_round):
            def rounds(i, carry, n0=n0, run_round=run_round):
                run_round([(i * subs_per_round + dr, n0 + dn)
                           for dr in range(subs_per_round) for dn in range(blks_per_round)])
                return carry

            if dil == subs_per_round:
                rounds(0, 0)
            else:
                lax.fori_loop(0, dil // subs_per_round, rounds, 0)

        if prev_dil > 1:
            for slot in range(3):
                for r_prev in range(prev_dil):
                    acc[3 * p_idx + slot, pl.ds(r_prev, prev_len, stride=prev_dil), :] = (
                        tmp[slot, r_prev * prev_len:(r_prev + 1) * prev_len, :])
        prev_dil = dil

    merge_rows = 256

    def merge(c, carry):
        rows = pl.ds(pl.multiple_of(c * merge_rows, merge_rows), merge_rows)
        maxes = [acc[3 * p + ROW_MAX, rows, :] for p in range(N_PAT)]
        top = functools.reduce(jnp.maximum, maxes)
        ws = [jnp.exp(mx - top) for mx in maxes]
        num = sum(w * acc[3 * p + PV, rows, :] for p, w in enumerate(ws))
        den = sum(w * acc[3 * p + ROW_SUM, rows, :] for p, w in enumerate(ws))
        o_ref[rows, :] = (num / den).astype(o_ref.dtype)
        return carry

    lax.fori_loop(0, seq // merge_rows, merge, 0)


def _dilated_attention(h):
    b, s, _ = h.shape
    blk = lambda off: pl.BlockSpec((None, s, LANES), lambda i, j: (i, 0, j + off))
    return pl.pallas_call(
        _dil_attn_kernel,
        grid=(b, HEAD_PAIRS),
        in_specs=[blk(0), blk(HEAD_PAIRS), blk(2 * HEAD_PAIRS)],
        out_specs=pl.BlockSpec((None, s, LANES), lambda i, j: (i, 0, j)),
        out_shape=jax.ShapeDtypeStruct((b, s, DIL_WIDTH), BF16),
        scratch_shapes=[pltpu.VMEM((2 * 3, s, LANES), F32),
                        pltpu.VMEM((2, s, LANES), BF16),
                        pltpu.VMEM((s, LANES), BF16),
                        pltpu.VMEM((s, LANES), BF16),
                        pltpu.VMEM((2 * BLOCK, 2 * BLOCK), F32),
                        pltpu.VMEM((N_PAT * 3, s, LANES), F32),
                        pltpu.VMEM((3, s, LANES), F32)],
        compiler_params=_params("parallel", "parallel"),
        name="dilated_attention",
    )(h, h, h)


def _mem_attn_kernel(q_ref, mk_ref, mv_ref, o_ref):
    seq = q_ref.shape[0]
    rows_per_step = 256
    lo = lax.broadcasted_iota(jnp.int32, (rows_per_step, LANES), 1) < HEAD_DIM

    def body(c, carry):
        rows = pl.ds(pl.multiple_of(c * rows_per_step, rows_per_step), rows_per_step)
        qb = (q_ref[rows, :].astype(F32) * Q_SCALE).astype(BF16)
        zero = jnp.zeros_like(qb)
        q2 = jnp.concatenate([jnp.where(lo, qb, zero), jnp.where(lo, zero, qb)], axis=0)
        mv = mv_ref[...]
        pv, _, den = _softmax_pv(q2, mk_ref[...], jnp.concatenate([mv, jnp.ones_like(mv)], axis=1), None)
        o = pv / den
        o_ref[rows, :] = jnp.where(lo, o[:rows_per_step], o[rows_per_step:]).astype(o_ref.dtype)
        return carry

    lax.fori_loop(0, seq // rows_per_step, body, 0, unroll=4)


def _memory_attention(qm_src, qm_col0, mkv):
    b, s, _ = qm_src.shape
    pairs = MEM_WIDTH // LANES
    return pl.pallas_call(
        _mem_attn_kernel,
        grid=(b, pairs),
        in_specs=[pl.BlockSpec((None, s, LANES), lambda i, j: (i, 0, j + qm_col0)),
                  pl.BlockSpec((None, N_MEM, LANES), lambda i, j: (i, 0, j)),
                  pl.BlockSpec((None, N_MEM, LANES), lambda i, j: (i, 0, j + pairs))],
        out_specs=pl.BlockSpec((None, s, LANES), lambda i, j: (i, 0, j)),
        out_shape=jax.ShapeDtypeStruct((b, s, MEM_WIDTH), BF16),
        compiler_params=_params("parallel", "parallel"),
        name="memory_attention",
    )(qm_src, mkv, mkv)


def _sgu_kernel(u_ref, v_ref, g_ref, b_ref, ws_ref, bias_ref, o_ref, vn_ref):
    tm = u_ref.shape[0]
    vn_ref[...] = _layer_norm(jax.nn.gelu(v_ref[...]), g_ref[...], b_ref[...]).astype(BF16)
    row = lax.broadcasted_iota(jnp.int32, (CHUNK, CHUNK), 0)
    col = lax.broadcasted_iota(jnp.int32, (CHUNK, CHUNK), 1)
    causal = row >= col
    lo = lax.broadcasted_iota(jnp.int32, (CHUNK, LANES), 1) < HEAD_DIM
    for gp in range(N_SGU_GROUPS // 2):
        cols = slice(gp * LANES, (gp + 1) * LANES)
        w0 = jnp.where(causal, ws_ref[2 * gp], 0.0).astype(BF16)
        w1 = jnp.where(causal, ws_ref[2 * gp + 1], 0.0).astype(BF16)
        bias = bias_ref[:, cols]
        for c in range(tm // CHUNK):
            rows = slice(c * CHUNK, (c + 1) * CHUNK)
            vc = vn_ref[rows, cols]
            mixed = jnp.where(lo, _dot(w0, vc), _dot(w1, vc)) + bias
            o_ref[rows, cols] = (jax.nn.gelu(u_ref[rows, cols]) * mixed).astype(o_ref.dtype)


def _sgu(uv, ln_g, ln_b, w_s, bias_full, tm):
    m = uv.shape[0]
    return pl.pallas_call(
        _sgu_kernel,
        grid=(m // tm,),
        in_specs=[pl.BlockSpec((tm, SGU_WIDTH), lambda i: (i, 0)),
                  pl.BlockSpec((tm, SGU_WIDTH), lambda i: (i, 1)),
                  pl.BlockSpec((1, SGU_WIDTH), lambda i: (0, 0)),
                  pl.BlockSpec((1, SGU_WIDTH), lambda i: (0, 0)),
                  pl.BlockSpec((N_SGU_GROUPS, CHUNK, CHUNK), lambda i: (0, 0, 0)),
                  pl.BlockSpec((CHUNK, SGU_WIDTH), lambda i: (0, 0))],
        out_specs=pl.BlockSpec((tm, SGU_WIDTH), lambda i: (i, 0)),
        out_shape=jax.ShapeDtypeStruct((m, SGU_WIDTH), BF16),
        scratch_shapes=[pltpu.VMEM((tm, SGU_WIDTH), BF16)],
        compiler_params=_params("parallel"),
        name="sgu",
    )(uv, uv, ln_g, ln_b, w_s, bias_full)


def _tail_kernel(mix_ref, mo_ref, x_ref, wa_ref, wb_ref, g1_ref, b1_ref, wg_ref, wu_ref, wd_ref,
                 g2_ref, b2_ref, o_ref, x1_ref, xb_ref, acc_ref):
    y = _dot(mix_ref[...], wa_ref[...]) + _dot(mo_ref[...], wb_ref[...])
    x1 = _layer_norm(DN_ALPHA * x_ref[...] + y, g1_ref[...], b1_ref[...])
    x1_ref[...] = x1
    xb_ref[...] = x1.astype(BF16)

    def down(c):
        xb = xb_ref[...]
        hid = jax.nn.silu(_dot(xb, wg_ref[c])) * _dot(xb, wu_ref[c])
        return _dot(hid.astype(BF16), wd_ref[c])

    acc_ref[...] = down(0)

    def body(c, carry):
        acc_ref[...] += down(c)
        return carry

    lax.fori_loop(1, wg_ref.shape[0], body, 0)
    o_ref[...] = _layer_norm(DN_ALPHA * x1_ref[...] + acc_ref[...], g2_ref[...], b2_ref[...])


def _layer_tail(mix, mo, x, w_out, g1, b1, wg, wu, wd, g2, b2, tm):
    m, d = x.shape
    wm = mix.shape[1]
    n_chunks = wg.shape[0]
    rows = lambda width: pl.BlockSpec((tm, width), lambda i: (i, 0))
    return pl.pallas_call(
        _tail_kernel,
        grid=(m // tm,),
        in_specs=[rows(wm), rows(MEM_WIDTH), rows(d),
                  pl.BlockSpec((wm, d), lambda i: (0, 0), pipeline_mode=pl.Buffered(1)),
                  pl.BlockSpec((MEM_WIDTH, d), lambda i: (wm // MEM_WIDTH, 0), pipeline_mode=pl.Buffered(1)),
                  _resident((1, d)), _resident((1, d)),
                  _resident((n_chunks, d, FF_CHUNK)), _resident((n_chunks, d, FF_CHUNK)),
                  _resident((n_chunks, FF_CHUNK, d)),
                  _resident((1, d)), _resident((1, d))],
        out_specs=rows(d),
        out_shape=jax.ShapeDtypeStruct((m, d), F32),
        scratch_shapes=[pltpu.VMEM((tm, d), F32), pltpu.VMEM((tm, d), BF16), pltpu.VMEM((tm, d), F32)],
        compiler_params=_params("parallel"),
        name="layer_tail",
    )(mix, mo, x, w_out, w_out, g1, b1, wg, wu, wd, g2, b2)


def _column_chunks(w):
    k, f = w.shape
    return w.astype(BF16).reshape(k, f // FF_CHUNK, FF_CHUNK).transpose(1, 0, 2)


def kernel(x, mem, a_w_in, b_w_in, sgu_ln_g, sgu_ln_b, sgu_w_s, sgu_b_s, w_mem_kv, w_out,
           ln_mix_g, ln_mix_b, w_gate, w_up, w_down, ln_ffn_g, ln_ffn_b):
    bsz, seq, d = x.shape
    m = bsz * seq
    xf = x.reshape(m, d)
    memf = mem.reshape(bsz * N_MEM, d)
    row = lambda a: a.reshape(1, -1)
    for i in range(DEPTH):
        j = i // 2
        (mkv,) = _project(memf, w_mem_kv[i].astype(BF16), [(0, 2 * MEM_WIDTH)], [BF16], tm=512)
        mkv = mkv.reshape(bsz, N_MEM, 2 * MEM_WIDTH)
        if i % 2 == 0:
            (h,) = _project(xf, a_w_in[j].astype(BF16), [(0, 3 * DIL_WIDTH + MEM_WIDTH)], [BF16], tm=512)
            h = h.reshape(bsz, seq, -1)
            mix = _dilated_attention(h).reshape(m, DIL_WIDTH)
            mo = _memory_attention(h, 3 * HEAD_PAIRS, mkv)
        else:
            uv, qm = _project(xf, b_w_in[j].astype(BF16),
                              [(0, 2 * SGU_WIDTH), (2 * SGU_WIDTH, MEM_WIDTH)], [F32, BF16], tm=512)
            bias_full = jnp.repeat(sgu_b_s[j].T, HEAD_DIM, axis=1)
            mix = _sgu(uv, row(sgu_ln_g[j]), row(sgu_ln_b[j]), sgu_w_s[j], bias_full, tm=512)
            mo = _memory_attention(qm.reshape(bsz, seq, MEM_WIDTH), 0, mkv)
        wd = w_down[i].astype(BF16).reshape(-1, FF_CHUNK, d)
        xf = _layer_tail(mix, mo.reshape(m, MEM_WIDTH), xf, w_out[i].astype(BF16),
                         row(ln_mix_g[i]), row(ln_mix_b[i]),
                         _column_chunks(w_gate[i]), _column_chunks(w_up[i]), wd,
                         row(ln_ffn_g[i]), row(ln_ffn_b[i]), tm=512)
    return xf.reshape(bsz, seq, d)
```

```python
import functools

import jax
import jax.numpy as jnp
from jax import lax
from jax.experimental import pallas as pl
from jax.experimental.pallas import tpu as pltpu

F32 = jnp.float32
BF16 = jnp.bfloat16

D_MODEL = 1024
N_MEM = 256
HEAD_DIM = 64
N_DIL_HEADS = 12
DIL_WIDTH = N_DIL_HEADS * HEAD_DIM
DIL_PATTERNS = ((128, 1), (512, 4), (2048, 16))
BLOCK = 128
N_SGU_GROUPS = 12
SGU_WIDTH = N_SGU_GROUPS * HEAD_DIM
CHUNK = 128
MEM_WIDTH = 4 * HEAD_DIM
DEPTH = 4
DN_ALPHA = (2 * DEPTH) ** 0.25
LN_EPS = 1e-5

LANES = 128
HEAD_PAIRS = DIL_WIDTH // LANES
Q_SCALE = HEAD_DIM ** -0.5
VMEM_LIMIT = 56 * 1024 * 1024
FF_CHUNK = 256
TAIL_ROW_GROUPS = 2


def _params(*sem):
    return pltpu.CompilerParams(dimension_semantics=sem, vmem_limit_bytes=VMEM_LIMIT)


def _layer_norm(z, g, b):
    mu = jnp.mean(z, axis=-1, keepdims=True)
    zc = z - mu
    var = jnp.mean(zc * zc, axis=-1, keepdims=True)
    return zc * lax.rsqrt(var + LN_EPS) * g + b


def _aligned(idx, multiple):
    return idx if isinstance(idx, int) else pl.multiple_of(idx, multiple)


def _dot(a, b):
    return jnp.dot(a, b, preferred_element_type=F32)


def _dot_nt(a, b):
    return lax.dot_general(a, b, (((1,), (1,)), ((), ())), preferred_element_type=F32)


def _resident(shape):
    return pl.BlockSpec(shape, lambda *_: (0,) * len(shape), pipeline_mode=pl.Buffered(1))


def _softmax_pv(q2, kb, v1, bias):
    s = _dot_nt(q2, kb)
    if bias is not None:
        s = s + bias
    m = jnp.max(s, axis=-1, keepdims=True)
    pv = _dot(jnp.exp(s - m).astype(BF16), v1)
    return pv[:, :LANES], m, pv[:, LANES:]


def _proj_kernel(splits, x_ref, w_ref, *out_refs):
    acc = _dot(x_ref[...].astype(BF16), w_ref[...])
    for (start, width), o_ref in zip(splits, out_refs):
        o_ref[...] = acc[:, start:start + width].astype(o_ref.dtype)


def _project(x, w, splits, dtypes, tm):
    m, k = x.shape
    n = w.shape[1]
    return pl.pallas_call(
        functools.partial(_proj_kernel, splits),
        grid=(m // tm,),
        in_specs=[pl.BlockSpec((tm, k), lambda i: (i, 0)), _resident((k, n))],
        out_specs=[pl.BlockSpec((tm, width), lambda i: (i, 0)) for _, width in splits],
        out_shape=[jax.ShapeDtypeStruct((m, width), dt) for (_, width), dt in zip(splits, dtypes)],
        compiler_params=_params("parallel"),
        name="in_proj",
    )(x, w)


N_PAT = len(DIL_PATTERNS)
Q, K, V = range(3)
PV, ROW_MAX, ROW_SUM = range(3)
UNITS_PER_ROUND = 16


def _dil_attn_kernel(q_ref, k_ref, v_ref, o_ref, lay, q2, kp, vp, bias_ref, acc, tmp):
    seq = q_ref.shape[0]
    hp = pl.program_id(1)
    lo = lax.broadcasted_iota(jnp.int32, (BLOCK, LANES), 1) < HEAD_DIM

    lay[Q] = q_ref[...].astype(F32) * Q_SCALE
    lay[K] = k_ref[...].astype(F32)
    lay[V] = v_ref[...].astype(F32)

    row = lax.broadcasted_iota(jnp.int32, (2 * BLOCK, 2 * BLOCK), 0)
    kc = lax.broadcasted_iota(jnp.int32, (2 * BLOCK, 2 * BLOCK), 1)
    steps = (row & (BLOCK - 1)) + BLOCK - kc
    head = (jnp.full((2 * BLOCK, 2 * BLOCK), 2 * hp, jnp.int32) + (row >> 7)).astype(F32)
    slope = jnp.exp2(-8.0 * (head + 1.0) / N_DIL_HEADS)

    prev_dil = 1
    for p_idx, (window, dil) in enumerate(DIL_PATTERNS):
        sub_len = seq // dil
        n_blk = sub_len // BLOCK
        ratio = dil // prev_dil
        prev_len = seq // prev_dil
        src, dst = 3 * ((p_idx + 1) % 2), 3 * (p_idx % 2)
        assert prev_dil * ratio == dil and (prev_dil == 1 or p_idx == 2)

        valid = (steps >= 0) & (steps <= window // dil)
        bias_ref[...] = jnp.where(valid, -slope * (steps * dil).astype(F32), -jnp.inf)

        if ratio == 1:
            lo_all = lax.broadcasted_iota(jnp.int32, (seq, LANES), 1) < HEAD_DIM
            q2[0] = jnp.where(lo_all, lay[dst + Q], 0.0).astype(BF16)
            q2[1] = jnp.where(lo_all, 0.0, lay[dst + Q]).astype(BF16)
            k_src, v_src = k_ref, v_ref
        else:
            lo_sub = lax.broadcasted_iota(jnp.int32, (sub_len, LANES), 1) < HEAD_DIM
            for r_prev in range(prev_dil):
                for c in range(ratio):
                    rows = pl.ds(r_prev * prev_len + c, sub_len, stride=ratio)
                    out_rows = pl.ds((r_prev + prev_dil * c) * sub_len, sub_len)
                    qr, kr, vr = lay[src + Q, rows, :], lay[src + K, rows, :], lay[src + V, rows, :]
                    if p_idx + 1 < N_PAT:
                        lay[dst + Q, out_rows, :] = qr
                        lay[dst + K, out_rows, :] = kr
                        lay[dst + V, out_rows, :] = vr
                    q2[0, out_rows, :] = jnp.where(lo_sub, qr, 0.0).astype(BF16)
                    q2[1, out_rows, :] = jnp.where(lo_sub, 0.0, qr).astype(BF16)
                    kp[out_rows, :] = kr.astype(BF16)
                    vp[out_rows, :] = vr.astype(BF16)
            k_src, v_src = kp, vp

        def store_for(r, n, p_idx=p_idx, dil=dil, ratio=ratio, prev_dil=prev_dil, prev_len=prev_len):
            if dil == 1:
                ref, base, rows = acc, 3 * p_idx, pl.ds(_aligned(n * BLOCK, BLOCK), BLOCK)
            elif prev_dil == 1:
                ref, base, rows = acc, 3 * p_idx, pl.ds(n * BLOCK * dil + r, BLOCK, stride=dil)
            else:
                start = lax.rem(r, prev_dil) * prev_len + n * BLOCK * ratio + lax.div(r, prev_dil)
                ref, base, rows = tmp, 0, pl.ds(start, BLOCK, stride=ratio)

            def store(tiles):
                for slot, tile in enumerate(tiles):
                    ref[base + slot, rows, :] = tile
            return store

        def run_round(blocks, k_src=k_src, v_src=v_src, store_for=store_for, sub_len=sub_len):
            state = [dict() for _ in blocks]

            def scores(b):
                r, n = blocks[b]
                row0 = _aligned(r * sub_len + n * BLOCK, BLOCK)
                qq = q2[:, pl.ds(row0, BLOCK), :].reshape(2 * BLOCK, LANES)
                if n > 0:
                    rows, bias = pl.ds(_aligned(row0 - BLOCK, BLOCK), 2 * BLOCK), bias_ref[...]
                else:
                    rows, bias = pl.ds(row0, BLOCK), bias_ref[:, BLOCK:]
                state[b].update(rows=rows, s=_dot_nt(qq, k_src[rows, :]) + bias)

            def row_max(b):
                state[b]["m"] = jnp.max(state[b]["s"], axis=-1, keepdims=True)

            def weighted_values(b):
                st = state[b]
                p = jnp.exp(st.pop("s") - st["m"]).astype(BF16)
                vb = v_src[st["rows"], :]
                st["pv"] = _dot(p, jnp.concatenate([vb, jnp.ones_like(vb)], axis=1))

            def store(b):
                st = state[b]
                pv = st.pop("pv")
                stats = (pv[:, :LANES], st.pop("m"), pv[:, LANES:])
                store_for(*blocks[b])([jnp.where(lo, t[:BLOCK], t[BLOCK:]) for t in stats])

            stages = ((scores, 0), (row_max, 2), (weighted_values, 3), (store, 5))
            for t in range(len(blocks) + stages[-1][1]):
                for stage, lag in stages:
                    if 0 <= t - lag < len(blocks):
                        stage(t - lag)

        subs_per_round = max(1, UNITS_PER_ROUND // n_blk)
        blks_per_round = min(n_blk, UNITS_PER_ROUND)
        for n0 in range(0, n_blk, blks_per_round):
            def rounds(i, carry, n0=n0, run_round=run_round):
                run_round([(i * subs_per_round + dr, n0 + dn)
                           for dr in range(subs_per_round) for dn in range(blks_per_round)])
                return carry

            if dil == subs_per_round:
                rounds(0, 0)
            else:
                lax.fori_loop(0, dil // subs_per_round, rounds, 0)

        if prev_dil > 1:
            for slot in range(3):
                for r_prev in range(prev_dil):
                    acc[3 * p_idx + slot, pl.ds(r_prev, prev_len, stride=prev_dil), :] = (
                        tmp[slot, r_prev * prev_len:(r_prev + 1) * prev_len, :])
        prev_dil = dil

    merge_rows = 256

    def merge(c, carry):
        rows = pl.ds(pl.multiple_of(c * merge_rows, merge_rows), merge_rows)
        maxes = [acc[3 * p + ROW_MAX, rows, :] for p in range(N_PAT)]
        top = functools.reduce(jnp.maximum, maxes)
        ws = [jnp.exp(mx - top) for mx in maxes]
        num = sum(w * acc[3 * p + PV, rows, :] for p, w in enumerate(ws))
        den = sum(w * acc[3 * p + ROW_SUM, rows, :] for p, w in enumerate(ws))
        o_ref[rows, :] = (num / den).astype(o_ref.dtype)
        return carry

    lax.fori_loop(0, seq // merge_rows, merge, 0)


def _dilated_attention(h):
    b, s, _ = h.shape
    blk = lambda off: pl.BlockSpec((None, s, LANES), lambda i, j: (i, 0, j + off))
    return pl.pallas_call(
        _dil_attn_kernel,
        grid=(b, HEAD_PAIRS),
        in_specs=[blk(0), blk(HEAD_PAIRS), blk(2 * HEAD_PAIRS)],
        out_specs=pl.BlockSpec((None, s, LANES), lambda i, j: (i, 0, j)),
        out_shape=jax.ShapeDtypeStruct((b, s, DIL_WIDTH), BF16),
        scratch_shapes=[pltpu.VMEM((2 * 3, s, LANES), F32),
                        pltpu.VMEM((2, s, LANES), BF16),
                        pltpu.VMEM((s, LANES), BF16),
                        pltpu.VMEM((s, LANES), BF16),
                        pltpu.VMEM((2 * BLOCK, 2 * BLOCK), F32),
                        pltpu.VMEM((N_PAT * 3, s, LANES), F32),
                        pltpu.VMEM((3, s, LANES), F32)],
        compiler_params=_params("parallel", "parallel"),
        name="dilated_attention",
    )(h, h, h)


def _mem_attn_kernel(q_ref, mk_ref, mv_ref, o_ref):
    seq = q_ref.shape[0]
    rows_per_step = 256
    lo = lax.broadcasted_iota(jnp.int32, (rows_per_step, LANES), 1) < HEAD_DIM

    def body(c, carry):
        rows = pl.ds(pl.multiple_of(c * rows_per_step, rows_per_step), rows_per_step)
        qb = (q_ref[rows, :].astype(F32) * Q_SCALE).astype(BF16)
        zero = jnp.zeros_like(qb)
        q2 = jnp.concatenate([jnp.where(lo, qb, zero), jnp.where(lo, zero, qb)], axis=0)
        mv = mv_ref[...]
        pv, _, den = _softmax_pv(q2, mk_ref[...], jnp.concatenate([mv, jnp.ones_like(mv)], axis=1), None)
        o = pv / den
        o_ref[rows, :] = jnp.where(lo, o[:rows_per_step], o[rows_per_step:]).astype(o_ref.dtype)
        return carry

    lax.fori_loop(0, seq // rows_per_step, body, 0, unroll=4)


def _memory_attention(qm_src, qm_col0, mkv):
    b, s, _ = qm_src.shape
    pairs = MEM_WIDTH // LANES
    return pl.pallas_call(
        _mem_attn_kernel,
        grid=(b, pairs),
        in_specs=[pl.BlockSpec((None, s, LANES), lambda i, j: (i, 0, j + qm_col0)),
                  pl.BlockSpec((None, N_MEM, LANES), lambda i, j: (i, 0, j)),
                  pl.BlockSpec((None, N_MEM, LANES), lambda i, j: (i, 0, j + pairs))],
        out_specs=pl.BlockSpec((None, s, LANES), lambda i, j: (i, 0, j)),
        out_shape=jax.ShapeDtypeStruct((b, s, MEM_WIDTH), BF16),
        compiler_params=_params("parallel", "parallel"),
        name="memory_attention",
    )(qm_src, mkv, mkv)


def _sgu_kernel(u_ref, v_ref, g_ref, b_ref, ws_ref, bias_ref, o_ref, vn_ref):
    tm = u_ref.shape[0]
    vn_ref[...] = _layer_norm(jax.nn.gelu(v_ref[...]), g_ref[...], b_ref[...]).astype(BF16)
    row = lax.broadcasted_iota(jnp.int32, (CHUNK, CHUNK), 0)
    col = lax.broadcasted_iota(jnp.int32, (CHUNK, CHUNK), 1)
    causal = row >= col
    lo = lax.broadcasted_iota(jnp.int32, (CHUNK, LANES), 1) < HEAD_DIM
    for gp in range(N_SGU_GROUPS // 2):
        cols = slice(gp * LANES, (gp + 1) * LANES)
        w0 = jnp.where(causal, ws_ref[2 * gp], 0.0).astype(BF16)
        w1 = jnp.where(causal, ws_ref[2 * gp + 1], 0.0).astype(BF16)
        bias = bias_ref[:, cols]
        for c in range(tm // CHUNK):
            rows = slice(c * CHUNK, (c + 1) * CHUNK)
            vc = vn_ref[rows, cols]
            mixed = jnp.where(lo, _dot(w0, vc), _dot(w1, vc)) + bias
            o_ref[rows, cols] = (jax.nn.gelu(u_ref[rows, cols]) * mixed).astype(o_ref.dtype)


def _sgu(uv, ln_g, ln_b, w_s, bias_full, tm):
    m = uv.shape[0]
    return pl.pallas_call(
        _sgu_kernel,
        grid=(m // tm,),
        in_specs=[pl.BlockSpec((tm, SGU_WIDTH), lambda i: (i, 0)),
                  pl.BlockSpec((tm, SGU_WIDTH), lambda i: (i, 1)),
                  pl.BlockSpec((1, SGU_WIDTH), lambda i: (0, 0)),
                  pl.BlockSpec((1, SGU_WIDTH), lambda i: (0, 0)),
                  pl.BlockSpec((N_SGU_GROUPS, CHUNK, CHUNK), lambda i: (0, 0, 0)),
                  pl.BlockSpec((CHUNK, SGU_WIDTH), lambda i: (0, 0))],
        out_specs=pl.BlockSpec((tm, SGU_WIDTH), lambda i: (i, 0)),
        out_shape=jax.ShapeDtypeStruct((m, SGU_WIDTH), BF16),
        scratch_shapes=[pltpu.VMEM((tm, SGU_WIDTH), BF16)],
        compiler_params=_params("parallel"),
        name="sgu",
    )(uv, uv, ln_g, ln_b, w_s, bias_full)


def _tail_kernel(mix_ref, mo_ref, x_ref, wa_ref, wb_ref, g1_ref, b1_ref, wg_ref, wu_ref, wd_ref,
                 g2_ref, b2_ref, o_ref, x1_ref, xb_ref, hid_ref):
    tm = x_ref.shape[0]
    groups = [slice(r, r + tm // TAIL_ROW_GROUPS) for r in range(0, tm, tm // TAIL_ROW_GROUPS)]
    for rows in groups:
        y = _dot(mix_ref[rows, :], wa_ref[...]) + _dot(mo_ref[rows, :], wb_ref[...])
        x1 = _layer_norm(DN_ALPHA * x_ref[rows, :] + y, g1_ref[...], b1_ref[...])
        x1_ref[rows, :] = x1
        xb_ref[rows, :] = x1.astype(BF16)
    for c in range(wg_ref.shape[1] // FF_CHUNK):
        cols = slice(c * FF_CHUNK, (c + 1) * FF_CHUNK)
        for rows in groups:
            xb = xb_ref[rows, :]
            hid = jax.nn.silu(_dot(xb, wg_ref[:, cols])) * _dot(xb, wu_ref[:, cols])
            hid_ref[rows, cols] = hid.astype(BF16)
    for rows in groups:
        f = _dot(hid_ref[rows, :], wd_ref[...])
        o_ref[rows, :] = _layer_norm(DN_ALPHA * x1_ref[rows, :] + f, g2_ref[...], b2_ref[...])


def _layer_tail(mix, mo, x, w_out, g1, b1, wg, wu, wd, g2, b2, tm):
    m, d = x.shape
    wm = mix.shape[1]
    f = wg.shape[1]
    rows = lambda width: pl.BlockSpec((tm, width), lambda i: (i, 0))
    return pl.pallas_call(
        _tail_kernel,
        grid=(m // tm,),
        in_specs=[rows(wm), rows(MEM_WIDTH), rows(d),
                  pl.BlockSpec((wm, d), lambda i: (0, 0), pipeline_mode=pl.Buffered(1)),
                  pl.BlockSpec((MEM_WIDTH, d), lambda i: (wm // MEM_WIDTH, 0), pipeline_mode=pl.Buffered(1)),
                  _resident((1, d)), _resident((1, d)),
                  _resident((d, f)), _resident((d, f)), _resident((f, d)),
                  _resident((1, d)), _resident((1, d))],
        out_specs=rows(d),
        out_shape=jax.ShapeDtypeStruct((m, d), F32),
        scratch_shapes=[pltpu.VMEM((tm, d), F32), pltpu.VMEM((tm, d), BF16), pltpu.VMEM((tm, f), BF16)],
        compiler_params=_params("parallel"),
        name="layer_tail",
    )(mix, mo, x, w_out, w_out, g1, b1, wg, wu, wd, g2, b2)


def kernel(x, mem, a_w_in, b_w_in, sgu_ln_g, sgu_ln_b, sgu_w_s, sgu_b_s, w_mem_kv, w_out,
           ln_mix_g, ln_mix_b, w_gate, w_up, w_down, ln_ffn_g, ln_ffn_b):
    bsz, seq, d = x.shape
    m = bsz * seq
    xf = x.reshape(m, d)
    memf = mem.reshape(bsz * N_MEM, d)
    row = lambda a: a.reshape(1, -1)
    for i in range(DEPTH):
        j = i // 2
        (mkv,) = _project(memf, w_mem_kv[i].astype(BF16), [(0, 2 * MEM_WIDTH)], [BF16], tm=512)
        mkv = mkv.reshape(bsz, N_MEM, 2 * MEM_WIDTH)
        if i % 2 == 0:
            (h,) = _project(xf, a_w_in[j].astype(BF16), [(0, 3 * DIL_WIDTH + MEM_WIDTH)], [BF16], tm=512)
            h = h.reshape(bsz, seq, -1)
            mix = _dilated_attention(h).reshape(m, DIL_WIDTH)
            mo = _memory_attention(h, 3 * HEAD_PAIRS, mkv)
        else:
            uv, qm = _project(xf, b_w_in[j].astype(BF16),
                              [(0, 2 * SGU_WIDTH), (2 * SGU_WIDTH, MEM_WIDTH)], [F32, BF16], tm=512)
            bias_full = jnp.repeat(sgu_b_s[j].T, HEAD_DIM, axis=1)
            mix = _sgu(uv, row(sgu_ln_g[j]), row(sgu_ln_b[j]), sgu_w_s[j], bias_full, tm=512)
            mo = _memory_attention(qm.reshape(bsz, seq, MEM_WIDTH), 0, mkv)
        xf = _layer_tail(mix, mo.reshape(m, MEM_WIDTH), xf, w_out[i].astype(BF16),
                         row(ln_mix_g[i]), row(ln_mix_b[i]),
                         w_gate[i].astype(BF16), w_up[i].astype(BF16), w_down[i].astype(BF16),
                         row(ln_ffn_g[i]), row(ln_ffn_b[i]), tm=512)
    return xf.reshape(bsz, seq, d)
```

```python
import functools

import jax
import jax.numpy as jnp
from jax import lax
from jax.experimental import pallas as pl
from jax.experimental.pallas import tpu as pltpu

F32 = jnp.float32
BF16 = jnp.bfloat16

D_MODEL = 1024
N_MEM = 256
HEAD_DIM = 64
N_DIL_HEADS = 12
DIL_WIDTH = N_DIL_HEADS * HEAD_DIM
DIL_PATTERNS = ((128, 1), (512, 4), (2048, 16))
BLOCK = 128
N_SGU_GROUPS = 12
SGU_WIDTH = N_SGU_GROUPS * HEAD_DIM
CHUNK = 128
MEM_WIDTH = 4 * HEAD_DIM
DEPTH = 4
DN_ALPHA = (2 * DEPTH) ** 0.25
LN_EPS = 1e-5

LANES = 128
HEAD_PAIRS = DIL_WIDTH // LANES
Q_SCALE = HEAD_DIM ** -0.5
VMEM_LIMIT = 56 * 1024 * 1024
FF_CHUNK = 256
TAIL_ROW_GROUPS = 2
STAGE_LAGS = (0, 2, 3, 5)


def _params(*sem):
    return pltpu.CompilerParams(dimension_semantics=sem, vmem_limit_bytes=VMEM_LIMIT)


def _layer_norm(z, g, b):
    mu = jnp.mean(z, axis=-1, keepdims=True)
    zc = z - mu
    var = jnp.mean(zc * zc, axis=-1, keepdims=True)
    return zc * lax.rsqrt(var + LN_EPS) * g + b


def _aligned(idx, multiple):
    return idx if isinstance(idx, int) else pl.multiple_of(idx, multiple)


def _dot(a, b):
    return jnp.dot(a, b, preferred_element_type=F32)


def _dot_nt(a, b):
    return lax.dot_general(a, b, (((1,), (1,)), ((), ())), preferred_element_type=F32)


def _resident(shape):
    return pl.BlockSpec(shape, lambda *_: (0,) * len(shape), pipeline_mode=pl.Buffered(1))


def _skewed(n_items, stages):
    for t in range(n_items + STAGE_LAGS[-1]):
        for stage, lag in zip(stages, STAGE_LAGS):
            if 0 <= t - lag < n_items:
                stage(t - lag)


def _proj_kernel(splits, x_ref, w_ref, *out_refs):
    acc = _dot(x_ref[...].astype(BF16), w_ref[...])
    for (start, width), o_ref in zip(splits, out_refs):
        o_ref[...] = acc[:, start:start + width].astype(o_ref.dtype)


def _project(x, w, splits, dtypes, tm):
    m, k = x.shape
    n = w.shape[1]
    return pl.pallas_call(
        functools.partial(_proj_kernel, splits),
        grid=(m // tm,),
        in_specs=[pl.BlockSpec((tm, k), lambda i: (i, 0)), _resident((k, n))],
        out_specs=[pl.BlockSpec((tm, width), lambda i: (i, 0)) for _, width in splits],
        out_shape=[jax.ShapeDtypeStruct((m, width), dt) for (_, width), dt in zip(splits, dtypes)],
        compiler_params=_params("parallel"),
        name="in_proj",
    )(x, w)


N_PAT = len(DIL_PATTERNS)
Q, K, V = range(3)
PV, ROW_MAX, ROW_SUM = range(3)
UNITS_PER_ROUND = 16
PAIRS_PER_STEP = 2
assert HEAD_PAIRS % PAIRS_PER_STEP == 0


def _dil_attn_kernel(q_ref, k_ref, v_ref, o_ref, *scratch):
    for pair in range(PAIRS_PER_STEP):
        lanes = pl.ds(pair * LANES, LANES)
        _attend_pair(pl.program_id(1) * PAIRS_PER_STEP + pair, q_ref.at[:, lanes], k_ref.at[:, lanes],
                     v_ref.at[:, lanes], o_ref.at[:, lanes], *scratch)


def _attend_pair(hp, q_ref, k_ref, v_ref, o_ref, lay, q2, kp, vp, bias_ref, acc, tmp):
    seq = q_ref.shape[0]
    lo = lax.broadcasted_iota(jnp.int32, (BLOCK, LANES), 1) < HEAD_DIM

    lay[Q] = q_ref[...].astype(F32) * Q_SCALE
    lay[K] = k_ref[...].astype(F32)
    lay[V] = v_ref[...].astype(F32)

    row = lax.broadcasted_iota(jnp.int32, (2 * BLOCK, 2 * BLOCK), 0)
    kc = lax.broadcasted_iota(jnp.int32, (2 * BLOCK, 2 * BLOCK), 1)
    steps = (row & (BLOCK - 1)) + BLOCK - kc
    head = (jnp.full((2 * BLOCK, 2 * BLOCK), 2 * hp, jnp.int32) + (row >> 7)).astype(F32)
    slope = jnp.exp2(-8.0 * (head + 1.0) / N_DIL_HEADS)

    prev_dil = 1
    for p_idx, (window, dil) in enumerate(DIL_PATTERNS):
        sub_len = seq // dil
        n_blk = sub_len // BLOCK
        ratio = dil // prev_dil
        prev_len = seq // prev_dil
        src, dst = 3 * ((p_idx + 1) % 2), 3 * (p_idx % 2)
        assert prev_dil * ratio == dil and (prev_dil == 1 or p_idx == 2)

        valid = (steps >= 0) & (steps <= window // dil)
        bias_ref[...] = jnp.where(valid, -slope * (steps * dil).astype(F32), -jnp.inf)

        if ratio == 1:
            lo_all = lax.broadcasted_iota(jnp.int32, (seq, LANES), 1) < HEAD_DIM
            q2[0] = jnp.where(lo_all, lay[dst + Q], 0.0).astype(BF16)
            q2[1] = jnp.where(lo_all, 0.0, lay[dst + Q]).astype(BF16)
            k_src, v_src = k_ref, v_ref
        else:
            lo_sub = lax.broadcasted_iota(jnp.int32, (sub_len, LANES), 1) < HEAD_DIM
            for r_prev in range(prev_dil):
                for c in range(ratio):
                    rows = pl.ds(r_prev * prev_len + c, sub_len, stride=ratio)
                    out_rows = pl.ds((r_prev + prev_dil * c) * sub_len, sub_len)
                    qr, kr, vr = lay[src + Q, rows, :], lay[src + K, rows, :], lay[src + V, rows, :]
                    if p_idx + 1 < N_PAT:
                        lay[dst + Q, out_rows, :] = qr
                        lay[dst + K, out_rows, :] = kr
                        lay[dst + V, out_rows, :] = vr
                    q2[0, out_rows, :] = jnp.where(lo_sub, qr, 0.0).astype(BF16)
                    q2[1, out_rows, :] = jnp.where(lo_sub, 0.0, qr).astype(BF16)
                    kp[out_rows, :] = kr.astype(BF16)
                    vp[out_rows, :] = vr.astype(BF16)
            k_src, v_src = kp, vp

        def store_for(r, n, p_idx=p_idx, dil=dil, ratio=ratio, prev_dil=prev_dil, prev_len=prev_len):
            if dil == 1:
                ref, base, rows = acc, 3 * p_idx, pl.ds(_aligned(n * BLOCK, BLOCK), BLOCK)
            elif prev_dil == 1:
                ref, base, rows = acc, 3 * p_idx, pl.ds(n * BLOCK * dil + r, BLOCK, stride=dil)
            else:
                start = lax.rem(r, prev_dil) * prev_len + n * BLOCK * ratio + lax.div(r, prev_dil)
                ref, base, rows = tmp, 0, pl.ds(start, BLOCK, stride=ratio)

            def store(tiles):
                for slot, tile in enumerate(tiles):
                    ref[base + slot, rows, :] = tile
            return store

        def run_round(blocks, k_src=k_src, v_src=v_src, store_for=store_for, sub_len=sub_len):
            state = [dict() for _ in blocks]

            def scores(b):
                r, n = blocks[b]
                row0 = _aligned(r * sub_len + n * BLOCK, BLOCK)
                qq = q2[:, pl.ds(row0, BLOCK), :].reshape(2 * BLOCK, LANES)
                if n > 0:
                    rows, bias = pl.ds(_aligned(row0 - BLOCK, BLOCK), 2 * BLOCK), bias_ref[...]
                else:
                    rows, bias = pl.ds(row0, BLOCK), bias_ref[:, BLOCK:]
                state[b].update(rows=rows, s=_dot_nt(qq, k_src[rows, :]) + bias)

            def row_max(b):
                state[b]["m"] = jnp.max(state[b]["s"], axis=-1, keepdims=True)

            def weighted_values(b):
                st = state[b]
                p = jnp.exp(st.pop("s") - st["m"]).astype(BF16)
                vb = v_src[st["rows"], :]
                st["pv"] = _dot(p, jnp.concatenate([vb, jnp.ones_like(vb)], axis=1))

            def store(b):
                st = state[b]
                pv = st.pop("pv")
                stats = (pv[:, :LANES], st.pop("m"), pv[:, LANES:])
                store_for(*blocks[b])([jnp.where(lo, t[:BLOCK], t[BLOCK:]) for t in stats])

            _skewed(len(blocks), (scores, row_max, weighted_values, store))

        subs_per_round = max(1, UNITS_PER_ROUND // n_blk)
        blks_per_round = min(n_blk, UNITS_PER_ROUND)
        for n0 in range(0, n_blk, blks_per_round):
            def rounds(i, carry, n0=n0, run_round=run_round):
                run_round([(i * subs_per_round + dr, n0 + dn)
                           for dr in range(subs_per_round) for dn in range(blks_per_round)])
                return carry

            if dil == subs_per_round:
                rounds(0, 0)
            else:
                lax.fori_loop(0, dil // subs_per_round, rounds, 0)

        if prev_dil > 1:
            for slot in range(3):
                for r_prev in range(prev_dil):
                    acc[3 * p_idx + slot, pl.ds(r_prev, prev_len, stride=prev_dil), :] = (
                        tmp[slot, r_prev * prev_len:(r_prev + 1) * prev_len, :])
        prev_dil = dil

    merge_rows = 256

    def merge(c, carry):
        rows = pl.ds(pl.multiple_of(c * merge_rows, merge_rows), merge_rows)
        maxes = [acc[3 * p + ROW_MAX, rows, :] for p in range(N_PAT)]
        top = functools.reduce(jnp.maximum, maxes)
        ws = [jnp.exp(mx - top) for mx in maxes]
        num = sum(w * acc[3 * p + PV, rows, :] for p, w in enumerate(ws))
        den = sum(w * acc[3 * p + ROW_SUM, rows, :] for p, w in enumerate(ws))
        o_ref[rows, :] = (num / den).astype(o_ref.dtype)
        return carry

    lax.fori_loop(0, seq // merge_rows, merge, 0)


def _dilated_attention(h):
    b, s, _ = h.shape
    steps = HEAD_PAIRS // PAIRS_PER_STEP
    blk = lambda off: pl.BlockSpec((None, s, PAIRS_PER_STEP * LANES), lambda i, j: (i, 0, j + off))
    return pl.pallas_call(
        _dil_attn_kernel,
        grid=(b, steps),
        in_specs=[blk(0), blk(steps), blk(2 * steps)],
        out_specs=blk(0),
        out_shape=jax.ShapeDtypeStruct((b, s, DIL_WIDTH), BF16),
        scratch_shapes=[pltpu.VMEM((2 * 3, s, LANES), F32),
                        pltpu.VMEM((2, s, LANES), BF16),
                        pltpu.VMEM((s, LANES), BF16),
                        pltpu.VMEM((s, LANES), BF16),
                        pltpu.VMEM((2 * BLOCK, 2 * BLOCK), F32),
                        pltpu.VMEM((N_PAT * 3, s, LANES), F32),
                        pltpu.VMEM((3, s, LANES), F32)],
        compiler_params=_params("parallel", "parallel"),
        name="dilated_attention",
    )(h, h, h)


def _mem_attn_kernel(q_ref, mk_ref, mv_ref, o_ref):
    seq = q_ref.shape[0]
    rows_per_item = 256
    lo = lax.broadcasted_iota(jnp.int32, (rows_per_item, LANES), 1) < HEAD_DIM
    state = [dict() for _ in range(seq // rows_per_item)]

    def scores(b):
        rows = slice(b * rows_per_item, (b + 1) * rows_per_item)
        qb = (q_ref[rows, :].astype(F32) * Q_SCALE).astype(BF16)
        zero = jnp.zeros_like(qb)
        q2 = jnp.concatenate([jnp.where(lo, qb, zero), jnp.where(lo, zero, qb)], axis=0)
        state[b]["s"] = _dot_nt(q2, mk_ref[...])

    def row_max(b):
        state[b]["m"] = jnp.max(state[b]["s"], axis=-1, keepdims=True)

    def weighted_values(b):
        st = state[b]
        p = jnp.exp(st.pop("s") - st.pop("m")).astype(BF16)
        mv = mv_ref[...]
        st["pv"] = _dot(p, jnp.concatenate([mv, jnp.ones_like(mv)], axis=1))

    def store(b):
        pv = state[b].pop("pv")
        o = pv[:, :LANES] / pv[:, LANES:]
        rows = slice(b * rows_per_item, (b + 1) * rows_per_item)
        o_ref[rows, :] = jnp.where(lo, o[:rows_per_item], o[rows_per_item:]).astype(o_ref.dtype)

    _skewed(len(state), (scores, row_max, weighted_values, store))


def _memory_attention(qm_src, qm_col0, mkv):
    b, s, _ = qm_src.shape
    pairs = MEM_WIDTH // LANES
    return pl.pallas_call(
        _mem_attn_kernel,
        grid=(b, pairs),
        in_specs=[pl.BlockSpec((None, s, LANES), lambda i, j: (i, 0, j + qm_col0)),
                  pl.BlockSpec((None, N_MEM, LANES), lambda i, j: (i, 0, j)),
                  pl.BlockSpec((None, N_MEM, LANES), lambda i, j: (i, 0, j + pairs))],
        out_specs=pl.BlockSpec((None, s, LANES), lambda i, j: (i, 0, j)),
        out_shape=jax.ShapeDtypeStruct((b, s, MEM_WIDTH), BF16),
        compiler_params=_params("parallel", "parallel"),
        name="memory_attention",
    )(qm_src, mkv, mkv)


GELU_C1 = (2.0 / 3.141592653589793) ** 0.5
GELU_C2 = GELU_C1 * 0.044715


def _gelu(x):
    return x * (0.5 * jnp.tanh(x * (GELU_C1 + GELU_C2 * (x * x))) + 0.5)


def _sgu_kernel(u_ref, v_ref, g_ref, b_ref, ws_ref, bias_ref, o_ref, vn_ref):
    tm = u_ref.shape[0]
    vn_ref[...] = _layer_norm(_gelu(v_ref[...]), g_ref[...], b_ref[...]).astype(BF16)
    row = lax.broadcasted_iota(jnp.int32, (CHUNK, CHUNK), 0)
    col = lax.broadcasted_iota(jnp.int32, (CHUNK, CHUNK), 1)
    causal = row >= col
    lo = lax.broadcasted_iota(jnp.int32, (CHUNK, LANES), 1) < HEAD_DIM
    for gp in range(N_SGU_GROUPS // 2):
        cols = slice(gp * LANES, (gp + 1) * LANES)
        w0 = jnp.where(causal, ws_ref[2 * gp], 0.0).astype(BF16)
        w1 = jnp.where(causal, ws_ref[2 * gp + 1], 0.0).astype(BF16)
        bias = bias_ref[:, cols]
        for c in range(tm // CHUNK):
            rows = slice(c * CHUNK, (c + 1) * CHUNK)
            vc = vn_ref[rows, cols]
            mixed = jnp.where(lo, _dot(w0, vc), _dot(w1, vc)) + bias
            o_ref[rows, cols] = (_gelu(u_ref[rows, cols]) * mixed).astype(o_ref.dtype)


def _sgu(uv, ln_g, ln_b, w_s, bias_full, tm):
    m = uv.shape[0]
    return pl.pallas_call(
        _sgu_kernel,
        grid=(m // tm,),
        in_specs=[pl.BlockSpec((tm, SGU_WIDTH), lambda i: (i, 0)),
                  pl.BlockSpec((tm, SGU_WIDTH), lambda i: (i, 1)),
                  pl.BlockSpec((1, SGU_WIDTH), lambda i: (0, 0)),
                  pl.BlockSpec((1, SGU_WIDTH), lambda i: (0, 0)),
                  pl.BlockSpec((N_SGU_GROUPS, CHUNK, CHUNK), lambda i: (0, 0, 0)),
                  pl.BlockSpec((CHUNK, SGU_WIDTH), lambda i: (0, 0))],
        out_specs=pl.BlockSpec((tm, SGU_WIDTH), lambda i: (i, 0)),
        out_shape=jax.ShapeDtypeStruct((m, SGU_WIDTH), BF16),
        scratch_shapes=[pltpu.VMEM((tm, SGU_WIDTH), BF16)],
        compiler_params=_params("parallel"),
        name="sgu",
    )(uv, uv, ln_g, ln_b, w_s, bias_full)


def _tail_kernel(mix_ref, mo_ref, x_ref, wa_ref, wb_ref, g1_ref, b1_ref, wg_ref, wu_ref, wd_ref,
                 g2_ref, b2_ref, o_ref, x1_ref, xb_ref, hid_ref):
    tm = x_ref.shape[0]
    groups = [slice(r, r + tm // TAIL_ROW_GROUPS) for r in range(0, tm, tm // TAIL_ROW_GROUPS)]
    for rows in groups:
        y = _dot(mix_ref[rows, :], wa_ref[...]) + _dot(mo_ref[rows, :], wb_ref[...])
        x1 = _layer_norm(DN_ALPHA * x_ref[rows, :] + y, g1_ref[...], b1_ref[...])
        x1_ref[rows, :] = x1
        xb_ref[rows, :] = x1.astype(BF16)
    for c in range(wg_ref.shape[1] // FF_CHUNK):
        cols = slice(c * FF_CHUNK, (c + 1) * FF_CHUNK)
        for rows in groups:
            xb = xb_ref[rows, :]
            hid = jax.nn.silu(_dot(xb, wg_ref[:, cols])) * _dot(xb, wu_ref[:, cols])
            hid_ref[rows, cols] = hid.astype(BF16)
    for rows in groups:
        f = _dot(hid_ref[rows, :], wd_ref[...])
        o_ref[rows, :] = _layer_norm(DN_ALPHA * x1_ref[rows, :] + f, g2_ref[...], b2_ref[...])


def _layer_tail(mix, mo, x, w_out, g1, b1, wg, wu, wd, g2, b2, tm):
    m, d = x.shape
    wm = mix.shape[1]
    f = wg.shape[1]
    rows = lambda width: pl.BlockSpec((tm, width), lambda i: (i, 0))
    return pl.pallas_call(
        _tail_kernel,
        grid=(m // tm,),
        in_specs=[rows(wm), rows(MEM_WIDTH), rows(d),
                  pl.BlockSpec((wm, d), lambda i: (0, 0), pipeline_mode=pl.Buffered(1)),
                  pl.BlockSpec((MEM_WIDTH, d), lambda i: (wm // MEM_WIDTH, 0), pipeline_mode=pl.Buffered(1)),
                  _resident((1, d)), _resident((1, d)),
                  _resident((d, f)), _resident((d, f)), _resident((f, d)),
                  _resident((1, d)), _resident((1, d))],
        out_specs=rows(d),
        out_shape=jax.ShapeDtypeStruct((m, d), F32),
        scratch_shapes=[pltpu.VMEM((tm, d), F32), pltpu.VMEM((tm, d), BF16), pltpu.VMEM((tm, f), BF16)],
        compiler_params=_params("parallel"),
        name="layer_tail",
    )(mix, mo, x, w_out, w_out, g1, b1, wg, wu, wd, g2, b2)


def kernel(x, mem, a_w_in, b_w_in, sgu_ln_g, sgu_ln_b, sgu_w_s, sgu_b_s, w_mem_kv, w_out,
           ln_mix_g, ln_mix_b, w_gate, w_up, w_down, ln_ffn_g, ln_ffn_b):
    bsz, seq, d = x.shape
    m = bsz * seq
    xf = x.reshape(m, d)
    memf = mem.reshape(bsz * N_MEM, d)
    row = lambda a: a.reshape(1, -1)
    for i in range(DEPTH):
        j = i // 2
        (mkv,) = _project(memf, w_mem_kv[i].astype(BF16), [(0, 2 * MEM_WIDTH)], [BF16], tm=512)
        mkv = mkv.reshape(bsz, N_MEM, 2 * MEM_WIDTH)
        if i % 2 == 0:
            (h,) = _project(xf, a_w_in[j].astype(BF16), [(0, 3 * DIL_WIDTH + MEM_WIDTH)], [BF16], tm=1024)
            h = h.reshape(bsz, seq, -1)
            mix = _dilated_attention(h).reshape(m, DIL_WIDTH)
            mo = _memory_attention(h, 3 * HEAD_PAIRS, mkv)
        else:
            uv, qm = _project(xf, b_w_in[j].astype(BF16),
                              [(0, 2 * SGU_WIDTH), (2 * SGU_WIDTH, MEM_WIDTH)], [F32, BF16], tm=1024)
            bias_full = jnp.repeat(sgu_b_s[j].T, HEAD_DIM, axis=1)
            mix = _sgu(uv, row(sgu_ln_g[j]), row(sgu_ln_b[j]), sgu_w_s[j], bias_full, tm=512)
            mo = _memory_attention(qm.reshape(bsz, seq, MEM_WIDTH), 0, mkv)
        xf = _layer_tail(mix, mo.reshape(m, MEM_WIDTH), xf, w_out[i].astype(BF16),
                         row(ln_mix_g[i]), row(ln_mix_b[i]),
                         w_gate[i].astype(BF16), w_up[i].astype(BF16), w_down[i].astype(BF16),
                         row(ln_ffn_g[i]), row(ln_ffn_b[i]), tm=512)
    return xf.reshape(bsz, seq, d)
```

```python
import functools

import jax
import jax.numpy as jnp
from jax import lax
from jax.experimental import pallas as pl
from jax.experimental.pallas import tpu as pltpu

F32 = jnp.float32
BF16 = jnp.bfloat16

D_MODEL = 1024
N_MEM = 256
HEAD_DIM = 64
N_DIL_HEADS = 12
DIL_WIDTH = N_DIL_HEADS * HEAD_DIM
DIL_PATTERNS = ((128, 1), (512, 4), (2048, 16))
BLOCK = 128
N_SGU_GROUPS = 12
SGU_WIDTH = N_SGU_GROUPS * HEAD_DIM
CHUNK = 128
MEM_WIDTH = 4 * HEAD_DIM
DEPTH = 4
DN_ALPHA = (2 * DEPTH) ** 0.25
LN_EPS = 1e-5

LANES = 128
HEAD_PAIRS = DIL_WIDTH // LANES
Q_SCALE = HEAD_DIM ** -0.5
VMEM_LIMIT = 56 * 1024 * 1024
FF_CHUNK = 256
TAIL_ROW_GROUPS = 2
STAGE_LAGS = (0, 2, 3, 5)


def _params(*sem):
    return pltpu.CompilerParams(dimension_semantics=sem, vmem_limit_bytes=VMEM_LIMIT)


def _layer_norm(z, g, b):
    mu = jnp.mean(z, axis=-1, keepdims=True)
    zc = z - mu
    var = jnp.mean(zc * zc, axis=-1, keepdims=True)
    return zc * lax.rsqrt(var + LN_EPS) * g + b


def _aligned(idx, multiple):
    return idx if isinstance(idx, int) else pl.multiple_of(idx, multiple)


def _dot(a, b):
    return jnp.dot(a, b, preferred_element_type=F32)


def _dot_nt(a, b):
    return lax.dot_general(a, b, (((1,), (1,)), ((), ())), preferred_element_type=F32)


def _resident(shape):
    return pl.BlockSpec(shape, lambda *_: (0,) * len(shape), pipeline_mode=pl.Buffered(1))


def _skewed(n_items, stages):
    for t in range(n_items + STAGE_LAGS[-1]):
        for stage, lag in zip(stages, STAGE_LAGS):
            if 0 <= t - lag < n_items:
                stage(t - lag)


def _proj_kernel(splits, x_ref, w_ref, *out_refs):
    acc = _dot(x_ref[...].astype(BF16), w_ref[...])
    for (start, width, tiled), o_ref in zip(splits, out_refs):
        if tiled:
            for c in range(width // LANES):
                o_ref[c] = acc[:, start + c * LANES:start + (c + 1) * LANES].astype(o_ref.dtype)
        else:
            o_ref[...] = acc[:, start:start + width].astype(o_ref.dtype)


def _project(x, w, splits, dtypes, tm):
    m, k = x.shape
    n = w.shape[1]
    out_specs, out_shape = [], []
    for (_, width, tiled), dt in zip(splits, dtypes):
        if tiled:
            out_specs.append(pl.BlockSpec((width // LANES, tm, LANES), lambda i: (0, i, 0)))
            out_shape.append(jax.ShapeDtypeStruct((width // LANES, m, LANES), dt))
        else:
            out_specs.append(pl.BlockSpec((tm, width), lambda i: (i, 0)))
            out_shape.append(jax.ShapeDtypeStruct((m, width), dt))
    return pl.pallas_call(
        functools.partial(_proj_kernel, splits),
        grid=(m // tm,),
        in_specs=[pl.BlockSpec((tm, k), lambda i: (i, 0)), _resident((k, n))],
        out_specs=out_specs,
        out_shape=out_shape,
        compiler_params=_params("parallel"),
        name="in_proj",
    )(x, w)


N_PAT = len(DIL_PATTERNS)
Q, K, V = range(3)
PV, ROW_MAX, ROW_SUM = range(3)
UNITS_PER_ROUND = 16


def _dil_attn_kernel(q_ref, k_ref, v_ref, o_ref, *scratch):
    def pair_body(pair, carry):
        _attend_pair(pair, q_ref.at[pair], k_ref.at[pair], v_ref.at[pair], o_ref.at[pair], *scratch)
        return carry

    lax.fori_loop(0, q_ref.shape[0], pair_body, 0)


def _attend_pair(hp, q_ref, k_ref, v_ref, o_ref, lay, q2, kp, vp, bias_ref, acc, tmp):
    seq = q_ref.shape[0]
    lo = lax.broadcasted_iota(jnp.int32, (BLOCK, LANES), 1) < HEAD_DIM

    lay[Q] = q_ref[...].astype(F32) * Q_SCALE
    lay[K] = k_ref[...].astype(F32)
    lay[V] = v_ref[...].astype(F32)

    row = lax.broadcasted_iota(jnp.int32, (2 * BLOCK, 2 * BLOCK), 0)
    kc = lax.broadcasted_iota(jnp.int32, (2 * BLOCK, 2 * BLOCK), 1)
    steps = (row & (BLOCK - 1)) + BLOCK - kc
    head = (jnp.full((2 * BLOCK, 2 * BLOCK), 2 * hp, jnp.int32) + (row >> 7)).astype(F32)
    slope = jnp.exp2(-8.0 * (head + 1.0) / N_DIL_HEADS)

    prev_dil = 1
    for p_idx, (window, dil) in enumerate(DIL_PATTERNS):
        sub_len = seq // dil
        n_blk = sub_len // BLOCK
        ratio = dil // prev_dil
        prev_len = seq // prev_dil
        src, dst = 3 * ((p_idx + 1) % 2), 3 * (p_idx % 2)
        assert prev_dil * ratio == dil and (prev_dil == 1 or p_idx == 2)

        valid = (steps >= 0) & (steps <= window // dil)
        bias_ref[...] = jnp.where(valid, -slope * (steps * dil).astype(F32), -jnp.inf)

        if ratio == 1:
            lo_all = lax.broadcasted_iota(jnp.int32, (seq, LANES), 1) < HEAD_DIM
            q2[0] = jnp.where(lo_all, lay[dst + Q], 0.0).astype(BF16)
            q2[1] = jnp.where(lo_all, 0.0, lay[dst + Q]).astype(BF16)
            k_src, v_src = k_ref, v_ref
        else:
            lo_sub = lax.broadcasted_iota(jnp.int32, (sub_len, LANES), 1) < HEAD_DIM
            for r_prev in range(prev_dil):
                for c in range(ratio):
                    rows = pl.ds(r_prev * prev_len + c, sub_len, stride=ratio)
                    out_rows = pl.ds((r_prev + prev_dil * c) * sub_len, sub_len)
                    qr, kr, vr = lay[src + Q, rows, :], lay[src + K, rows, :], lay[src + V, rows, :]
                    if p_idx + 1 < N_PAT:
                        lay[dst + Q, out_rows, :] = qr
                        lay[dst + K, out_rows, :] = kr
                        lay[dst + V, out_rows, :] = vr
                    q2[0, out_rows, :] = jnp.where(lo_sub, qr, 0.0).astype(BF16)
                    q2[1, out_rows, :] = jnp.where(lo_sub, 0.0, qr).astype(BF16)
                    kp[out_rows, :] = kr.astype(BF16)
                    vp[out_rows, :] = vr.astype(BF16)
            k_src, v_src = kp, vp

        def store_for(r, n, p_idx=p_idx, dil=dil, ratio=ratio, prev_dil=prev_dil, prev_len=prev_len):
            if dil == 1:
                ref, base, rows = acc, 3 * p_idx, pl.ds(_aligned(n * BLOCK, BLOCK), BLOCK)
            elif prev_dil == 1:
                ref, base, rows = acc, 3 * p_idx, pl.ds(n * BLOCK * dil + r, BLOCK, stride=dil)
            else:
                start = lax.rem(r, prev_dil) * prev_len + n * BLOCK * ratio + lax.div(r, prev_dil)
                ref, base, rows = tmp, 0, pl.ds(start, BLOCK, stride=ratio)

            def store(tiles):
                for slot, tile in enumerate(tiles):
                    ref[base + slot, rows, :] = tile
            return store

        def run_round(blocks, k_src=k_src, v_src=v_src, store_for=store_for, sub_len=sub_len):
            state = [dict() for _ in blocks]

            def scores(b):
                r, n = blocks[b]
                row0 = _aligned(r * sub_len + n * BLOCK, BLOCK)
                qq = q2[:, pl.ds(row0, BLOCK), :].reshape(2 * BLOCK, LANES)
                if n > 0:
                    rows, bias = pl.ds(_aligned(row0 - BLOCK, BLOCK), 2 * BLOCK), bias_ref[...]
                else:
                    rows, bias = pl.ds(row0, BLOCK), bias_ref[:, BLOCK:]
                state[b].update(rows=rows, s=_dot_nt(qq, k_src[rows, :]) + bias)

            def row_max(b):
                state[b]["m"] = jnp.max(state[b]["s"], axis=-1, keepdims=True)

            def weighted_values(b):
                st = state[b]
                p = jnp.exp(st.pop("s") - st["m"]).astype(BF16)
                vb = v_src[st["rows"], :]
                st["pv"] = _dot(p, jnp.concatenate([vb, jnp.ones_like(vb)], axis=1))

            def store(b):
                st = state[b]
                pv = st.pop("pv")
                stats = (pv[:, :LANES], st.pop("m"), pv[:, LANES:])
                store_for(*blocks[b])([jnp.where(lo, t[:BLOCK], t[BLOCK:]) for t in stats])

            _skewed(len(blocks), (scores, row_max, weighted_values, store))

        subs_per_round = max(1, UNITS_PER_ROUND // n_blk)
        blks_per_round = min(n_blk, UNITS_PER_ROUND)
        for n0 in range(0, n_blk, blks_per_round):
            def rounds(i, carry, n0=n0, run_round=run_round):
                run_round([(i * subs_per_round + dr, n0 + dn)
                           for dr in range(subs_per_round) for dn in range(blks_per_round)])
                return carry

            if dil == subs_per_round:
                rounds(0, 0)
            else:
                lax.fori_loop(0, dil // subs_per_round, rounds, 0)

        if prev_dil > 1:
            for slot in range(3):
                for r_prev in range(prev_dil):
                    acc[3 * p_idx + slot, pl.ds(r_prev, prev_len, stride=prev_dil), :] = (
                        tmp[slot, r_prev * prev_len:(r_prev + 1) * prev_len, :])
        prev_dil = dil

    merge_rows = 256

    def merge(c, carry):
        rows = pl.ds(pl.multiple_of(c * merge_rows, merge_rows), merge_rows)
        maxes = [acc[3 * p + ROW_MAX, rows, :] for p in range(N_PAT)]
        top = functools.reduce(jnp.maximum, maxes)
        ws = [jnp.exp(mx - top) for mx in maxes]
        num = sum(w * acc[3 * p + PV, rows, :] for p, w in enumerate(ws))
        den = sum(w * acc[3 * p + ROW_SUM, rows, :] for p, w in enumerate(ws))
        o_ref[rows, :] = (num / den).astype(o_ref.dtype)
        return carry

    lax.fori_loop(0, seq // merge_rows, merge, 0)


def _dilated_attention(h):
    _, b, s, _ = h.shape
    blk = lambda group: pl.BlockSpec((HEAD_PAIRS, None, s, LANES), lambda i: (group, i, 0, 0))
    return pl.pallas_call(
        _dil_attn_kernel,
        grid=(b,),
        in_specs=[blk(0), blk(1), blk(2)],
        out_specs=blk(0),
        out_shape=jax.ShapeDtypeStruct((HEAD_PAIRS, b, s, LANES), BF16),
        scratch_shapes=[pltpu.VMEM((2 * 3, s, LANES), F32),
                        pltpu.VMEM((2, s, LANES), BF16),
                        pltpu.VMEM((s, LANES), BF16),
                        pltpu.VMEM((s, LANES), BF16),
                        pltpu.VMEM((2 * BLOCK, 2 * BLOCK), F32),
                        pltpu.VMEM((N_PAT * 3, s, LANES), F32),
                        pltpu.VMEM((3, s, LANES), F32)],
        compiler_params=_params("parallel"),
        name="dilated_attention",
    )(h, h, h)


MEM_PAIRS = MEM_WIDTH // LANES


def _mem_attn_kernel(q_ref, mkv_ref, o_ref):
    seq = q_ref.shape[1]
    rows_per_item = 256
    items_per_pair = seq // rows_per_item
    lo = lax.broadcasted_iota(jnp.int32, (rows_per_item, LANES), 1) < HEAD_DIM
    state = [dict() for _ in range(MEM_PAIRS * items_per_pair)]

    def place(b):
        pair, item = divmod(b, items_per_pair)
        return pair, slice(item * rows_per_item, (item + 1) * rows_per_item)

    def scores(b):
        pair, rows = place(b)
        qb = (q_ref[pair, rows, :].astype(F32) * Q_SCALE).astype(BF16)
        zero = jnp.zeros_like(qb)
        q2 = jnp.concatenate([jnp.where(lo, qb, zero), jnp.where(lo, zero, qb)], axis=0)
        state[b]["s"] = _dot_nt(q2, mkv_ref[:, pair * LANES:(pair + 1) * LANES])

    def row_max(b):
        state[b]["m"] = jnp.max(state[b]["s"], axis=-1, keepdims=True)

    def weighted_values(b):
        st = state[b]
        pair, _ = place(b)
        p = jnp.exp(st.pop("s") - st.pop("m")).astype(BF16)
        mv = mkv_ref[:, MEM_WIDTH + pair * LANES:MEM_WIDTH + (pair + 1) * LANES]
        st["pv"] = _dot(p, jnp.concatenate([mv, jnp.ones_like(mv)], axis=1))

    def store(b):
        pair, rows = place(b)
        pv = state[b].pop("pv")
        o = pv[:, :LANES] / pv[:, LANES:]
        o_ref[pair, rows, :] = jnp.where(lo, o[:rows_per_item], o[rows_per_item:]).astype(o_ref.dtype)

    _skewed(len(state), (scores, row_max, weighted_values, store))


def _memory_attention(q_tiles, first_tile, mkv):
    _, b, s, _ = q_tiles.shape
    assert first_tile % MEM_PAIRS == 0
    return pl.pallas_call(
        _mem_attn_kernel,
        grid=(b,),
        in_specs=[pl.BlockSpec((MEM_PAIRS, None, s, LANES), lambda i: (first_tile // MEM_PAIRS, i, 0, 0)),
                  pl.BlockSpec((None, N_MEM, 2 * MEM_WIDTH), lambda i: (i, 0, 0))],
        out_specs=pl.BlockSpec((MEM_PAIRS, None, s, LANES), lambda i: (0, i, 0, 0)),
        out_shape=jax.ShapeDtypeStruct((MEM_PAIRS, b, s, LANES), BF16),
        compiler_params=_params("parallel"),
        name="memory_attention",
    )(q_tiles, mkv)


GELU_C1 = (2.0 / 3.141592653589793) ** 0.5
GELU_C2 = GELU_C1 * 0.044715


def _gelu(x):
    return x * (0.5 * jnp.tanh(x * (GELU_C1 + GELU_C2 * (x * x))) + 0.5)


SGU_ROW_GROUPS = 2


def _proj_sgu_kernel(x_ref, w_ref, g_ref, b_ref, ws_ref, bias_ref, mix_ref, qm_ref, u_ref, vn_ref):
    tm = x_ref.shape[0]
    row = lax.broadcasted_iota(jnp.int32, (CHUNK, CHUNK), 0)
    col = lax.broadcasted_iota(jnp.int32, (CHUNK, CHUNK), 1)
    causal = row >= col
    lo = lax.broadcasted_iota(jnp.int32, (CHUNK, LANES), 1) < HEAD_DIM
    groups = [slice(r, r + tm // SGU_ROW_GROUPS) for r in range(0, tm, tm // SGU_ROW_GROUPS)]
    for rows in groups:
        h = _dot(x_ref[rows, :].astype(BF16), w_ref[...])
        u_ref[rows, :] = h[:, :SGU_WIDTH]
        vn_ref[rows, :] = _layer_norm(_gelu(h[:, SGU_WIDTH:2 * SGU_WIDTH]),
                                      g_ref[...], b_ref[...]).astype(BF16)
        for c in range(MEM_PAIRS):
            first = 2 * SGU_WIDTH + c * LANES
            qm_ref[c, rows, :] = h[:, first:first + LANES].astype(qm_ref.dtype)
    for gp in range(N_SGU_GROUPS // 2):
        cols = slice(gp * LANES, (gp + 1) * LANES)
        w0 = jnp.where(causal, ws_ref[2 * gp], 0.0).astype(BF16)
        w1 = jnp.where(causal, ws_ref[2 * gp + 1], 0.0).astype(BF16)
        bias = bias_ref[:, cols]
        for c in range(tm // CHUNK):
            rows = slice(c * CHUNK, (c + 1) * CHUNK)
            vc = vn_ref[rows, cols]
            mixed = jnp.where(lo, _dot(w0, vc), _dot(w1, vc)) + bias
            mix_ref[gp, rows, :] = (_gelu(u_ref[rows, cols]) * mixed).astype(mix_ref.dtype)


def _project_sgu(x, w, ln_g, ln_b, w_s, bias_full, tm):
    m, k = x.shape
    tiles = lambda n: pl.BlockSpec((n, tm, LANES), lambda i: (0, i, 0))
    return pl.pallas_call(
        _proj_sgu_kernel,
        grid=(m // tm,),
        in_specs=[pl.BlockSpec((tm, k), lambda i: (i, 0)), _resident(w.shape),
                  _resident((1, SGU_WIDTH)), _resident((1, SGU_WIDTH)),
                  _resident((N_SGU_GROUPS, CHUNK, CHUNK)), _resident((CHUNK, SGU_WIDTH))],
        out_specs=[tiles(SGU_WIDTH // LANES), tiles(MEM_PAIRS)],
        out_shape=[jax.ShapeDtypeStruct((SGU_WIDTH // LANES, m, LANES), BF16),
                   jax.ShapeDtypeStruct((MEM_PAIRS, m, LANES), BF16)],
        scratch_shapes=[pltpu.VMEM((tm, SGU_WIDTH), F32), pltpu.VMEM((tm, SGU_WIDTH), BF16)],
        compiler_params=_params("parallel"),
        name="in_proj_sgu",
    )(x, w, ln_g, ln_b, w_s, bias_full)


def _tail_kernel(mix_ref, mo_ref, x_ref, wo_ref, g1_ref, b1_ref, wg_ref, wu_ref, wd_ref,
                 g2_ref, b2_ref, o_ref, x1_ref, xb_ref, hid_ref):
    tm = x_ref.shape[0]
    groups = [slice(r, r + tm // TAIL_ROW_GROUPS) for r in range(0, tm, tm // TAIL_ROW_GROUPS)]
    for rows in groups:
        mixed = jnp.concatenate([ref[c, rows, :] for ref in (mix_ref, mo_ref) for c in range(ref.shape[0])],
                                axis=1)
        x1 = _layer_norm(DN_ALPHA * x_ref[rows, :] + _dot(mixed, wo_ref[...]), g1_ref[...], b1_ref[...])
        x1_ref[rows, :] = x1
        xb_ref[rows, :] = x1.astype(BF16)
    for c in range(wg_ref.shape[1] // FF_CHUNK):
        cols = slice(c * FF_CHUNK, (c + 1) * FF_CHUNK)
        for rows in groups:
            xb = xb_ref[rows, :]
            hid = jax.nn.silu(_dot(xb, wg_ref[:, cols])) * _dot(xb, wu_ref[:, cols])
            hid_ref[rows, cols] = hid.astype(BF16)
    for rows in groups:
        f = _dot(hid_ref[rows, :], wd_ref[...])
        o_ref[rows, :] = _layer_norm(DN_ALPHA * x1_ref[rows, :] + f, g2_ref[...], b2_ref[...])


def _layer_tail(mix, mo, x, w_out, g1, b1, wg, wu, wd, g2, b2, tm):
    m, d = x.shape
    f = wg.shape[1]
    tiles = lambda a: pl.BlockSpec((a.shape[0], tm, LANES), lambda i: (0, i, 0))
    rows = pl.BlockSpec((tm, d), lambda i: (i, 0))
    return pl.pallas_call(
        _tail_kernel,
        grid=(m // tm,),
        in_specs=[tiles(mix), tiles(mo), rows,
                  _resident((d, d)), _resident((1, d)), _resident((1, d)),
                  _resident((d, f)), _resident((d, f)), _resident((f, d)),
                  _resident((1, d)), _resident((1, d))],
        out_specs=rows,
        out_shape=jax.ShapeDtypeStruct((m, d), F32),
        scratch_shapes=[pltpu.VMEM((tm, d), F32), pltpu.VMEM((tm, d), BF16), pltpu.VMEM((tm, f), BF16)],
        compiler_params=_params("parallel"),
        name="layer_tail",
    )(mix, mo, x, w_out, g1, b1, wg, wu, wd, g2, b2)


def kernel(x, mem, a_w_in, b_w_in, sgu_ln_g, sgu_ln_b, sgu_w_s, sgu_b_s, w_mem_kv, w_out,
           ln_mix_g, ln_mix_b, w_gate, w_up, w_down, ln_ffn_g, ln_ffn_b):
    bsz, seq, d = x.shape
    m = bsz * seq
    xf = x.reshape(m, d)
    memf = mem.reshape(bsz * N_MEM, d)
    row = lambda a: a.reshape(1, -1)
    for i in range(DEPTH):
        j = i // 2
        (mkv,) = _project(memf, w_mem_kv[i].astype(BF16), [(0, 2 * MEM_WIDTH, False)], [BF16], tm=512)
        mkv = mkv.reshape(bsz, N_MEM, 2 * MEM_WIDTH)
        per_batch = lambda t: t.reshape(t.shape[0], bsz, seq, LANES)
        if i % 2 == 0:
            (h,) = _project(xf, a_w_in[j].astype(BF16), [(0, 3 * DIL_WIDTH + MEM_WIDTH, True)], [BF16],
                            tm=1024)
            h = per_batch(h)
            mix = _dilated_attention(h).reshape(HEAD_PAIRS, m, LANES)
            mo = _memory_attention(h, 3 * HEAD_PAIRS, mkv)
        else:
            bias_full = jnp.repeat(sgu_b_s[j].T, HEAD_DIM, axis=1)
            mix, qm = _project_sgu(xf, b_w_in[j].astype(BF16), row(sgu_ln_g[j]), row(sgu_ln_b[j]),
                                   sgu_w_s[j], bias_full, tm=512)
            mo = _memory_attention(per_batch(qm), 0, mkv)
        xf = _layer_tail(mix, mo.reshape(MEM_PAIRS, m, LANES), xf, w_out[i].astype(BF16),
                         row(ln_mix_g[i]), row(ln_mix_b[i]),
                         w_gate[i].astype(BF16), w_up[i].astype(BF16), w_down[i].astype(BF16),
                         row(ln_ffn_g[i]), row(ln_ffn_b[i]), tm=512)
    return xf.reshape(bsz, seq, d)
```

```python
import functools

import jax
import jax.numpy as jnp
from jax import lax
from jax.experimental import pallas as pl
from jax.experimental.pallas import tpu as pltpu

F32 = jnp.float32
BF16 = jnp.bfloat16

D_MODEL = 1024
N_MEM = 256
HEAD_DIM = 64
N_DIL_HEADS = 12
DIL_WIDTH = N_DIL_HEADS * HEAD_DIM
DIL_PATTERNS = ((128, 1), (512, 4), (2048, 16))
BLOCK = 128
N_SGU_GROUPS = 12
SGU_WIDTH = N_SGU_GROUPS * HEAD_DIM
CHUNK = 128
MEM_WIDTH = 4 * HEAD_DIM
DEPTH = 4
DN_ALPHA = (2 * DEPTH) ** 0.25
LN_EPS = 1e-5

LANES = 128
HEAD_PAIRS = DIL_WIDTH // LANES
Q_SCALE = HEAD_DIM ** -0.5
VMEM_LIMIT = 56 * 1024 * 1024
FF_CHUNK = 256
TAIL_ROW_GROUPS = 2
STAGE_LAGS = (0, 1, 2, 3)


def _params(*sem):
    return pltpu.CompilerParams(dimension_semantics=sem, vmem_limit_bytes=VMEM_LIMIT)


def _layer_norm(z, g, b):
    mu = jnp.mean(z, axis=-1, keepdims=True)
    zc = z - mu
    var = jnp.mean(zc * zc, axis=-1, keepdims=True)
    return zc * lax.rsqrt(var + LN_EPS) * g + b


def _aligned(idx, multiple):
    return idx if isinstance(idx, int) else pl.multiple_of(idx, multiple)


def _dot(a, b):
    return jnp.dot(a, b, preferred_element_type=F32)


def _dot_nt(a, b):
    return lax.dot_general(a, b, (((1,), (1,)), ((), ())), preferred_element_type=F32)


def _resident(shape):
    return pl.BlockSpec(shape, lambda *_: (0,) * len(shape), pipeline_mode=pl.Buffered(1))


def _skewed(n_items, stages):
    for t in range(n_items + STAGE_LAGS[-1]):
        for stage, lag in zip(stages, STAGE_LAGS):
            if 0 <= t - lag < n_items:
                stage(t - lag)


def _proj_kernel(splits, x_ref, w_ref, *out_refs):
    acc = _dot(x_ref[...].astype(BF16), w_ref[...])
    for (start, width, tiled), o_ref in zip(splits, out_refs):
        if tiled:
            for c in range(width // LANES):
                o_ref[c] = acc[:, start + c * LANES:start + (c + 1) * LANES].astype(o_ref.dtype)
        else:
            o_ref[...] = acc[:, start:start + width].astype(o_ref.dtype)


def _project(x, w, splits, dtypes, tm):
    m, k = x.shape
    n = w.shape[1]
    out_specs, out_shape = [], []
    for (_, width, tiled), dt in zip(splits, dtypes):
        if tiled:
            out_specs.append(pl.BlockSpec((width // LANES, tm, LANES), lambda i: (0, i, 0)))
            out_shape.append(jax.ShapeDtypeStruct((width // LANES, m, LANES), dt))
        else:
            out_specs.append(pl.BlockSpec((tm, width), lambda i: (i, 0)))
            out_shape.append(jax.ShapeDtypeStruct((m, width), dt))
    return pl.pallas_call(
        functools.partial(_proj_kernel, splits),
        grid=(m // tm,),
        in_specs=[pl.BlockSpec((tm, k), lambda i: (i, 0)), _resident((k, n))],
        out_specs=out_specs,
        out_shape=out_shape,
        compiler_params=_params("parallel"),
        name="in_proj",
    )(x, w)


N_PAT = len(DIL_PATTERNS)
Q, K, V = range(3)
PV, ROW_MAX, ROW_SUM = range(3)
UNITS_PER_ROUND = 16


def _dil_attn_kernel(q_ref, k_ref, v_ref, o_ref, *scratch):
    def pair_body(pair, carry):
        _attend_pair(pair, q_ref.at[pair], k_ref.at[pair], v_ref.at[pair], o_ref.at[pair], *scratch)
        return carry

    lax.fori_loop(0, q_ref.shape[0], pair_body, 0)


def _attend_pair(hp, q_ref, k_ref, v_ref, o_ref, lay, q2, kp, vp, bias_ref, acc, tmp):
    seq = q_ref.shape[0]
    lo = lax.broadcasted_iota(jnp.int32, (BLOCK, LANES), 1) < HEAD_DIM

    lay[Q] = q_ref[...].astype(F32) * Q_SCALE
    lay[K] = k_ref[...].astype(F32)
    lay[V] = v_ref[...].astype(F32)

    row = lax.broadcasted_iota(jnp.int32, (2 * BLOCK, 2 * BLOCK), 0)
    kc = lax.broadcasted_iota(jnp.int32, (2 * BLOCK, 2 * BLOCK), 1)
    steps = (row & (BLOCK - 1)) + BLOCK - kc
    head = (jnp.full((2 * BLOCK, 2 * BLOCK), 2 * hp, jnp.int32) + (row >> 7)).astype(F32)
    slope = jnp.exp2(-8.0 * (head + 1.0) / N_DIL_HEADS)

    prev_dil = 1
    for p_idx, (window, dil) in enumerate(DIL_PATTERNS):
        sub_len = seq // dil
        n_blk = sub_len // BLOCK
        ratio = dil // prev_dil
        prev_len = seq // prev_dil
        src, dst = 3 * ((p_idx + 1) % 2), 3 * (p_idx % 2)
        assert prev_dil * ratio == dil and (prev_dil == 1 or p_idx == 2)

        valid = (steps >= 0) & (steps <= window // dil)
        bias_ref[...] = jnp.where(valid, -slope * (steps * dil).astype(F32), -jnp.inf)

        if ratio == 1:
            lo_all = lax.broadcasted_iota(jnp.int32, (seq, LANES), 1) < HEAD_DIM
            q2[0] = jnp.where(lo_all, lay[dst + Q], 0.0).astype(BF16)
            q2[1] = jnp.where(lo_all, 0.0, lay[dst + Q]).astype(BF16)
            k_src, v_src = k_ref, v_ref
        else:
            lo_sub = lax.broadcasted_iota(jnp.int32, (sub_len, LANES), 1) < HEAD_DIM
            for r_prev in range(prev_dil):
                for c in range(ratio):
                    rows = pl.ds(r_prev * prev_len + c, sub_len, stride=ratio)
                    out_rows = pl.ds((r_prev + prev_dil * c) * sub_len, sub_len)
                    qr, kr, vr = lay[src + Q, rows, :], lay[src + K, rows, :], lay[src + V, rows, :]
                    if p_idx + 1 < N_PAT:
                        lay[dst + Q, out_rows, :] = qr
                        lay[dst + K, out_rows, :] = kr
                        lay[dst + V, out_rows, :] = vr
                    q2[0, out_rows, :] = jnp.where(lo_sub, qr, 0.0).astype(BF16)
                    q2[1, out_rows, :] = jnp.where(lo_sub, 0.0, qr).astype(BF16)
                    kp[out_rows, :] = kr.astype(BF16)
                    vp[out_rows, :] = vr.astype(BF16)
            k_src, v_src = kp, vp

        def store_for(r, n, p_idx=p_idx, dil=dil, ratio=ratio, prev_dil=prev_dil, prev_len=prev_len):
            if dil == 1:
                ref, base, rows = acc, 3 * p_idx, pl.ds(_aligned(n * BLOCK, BLOCK), BLOCK)
            elif prev_dil == 1:
                ref, base, rows = acc, 3 * p_idx, pl.ds(n * BLOCK * dil + r, BLOCK, stride=dil)
            else:
                start = lax.rem(r, prev_dil) * prev_len + n * BLOCK * ratio + lax.div(r, prev_dil)
                ref, base, rows = tmp, 0, pl.ds(start, BLOCK, stride=ratio)

            def store(tiles):
                for slot, tile in enumerate(tiles):
                    ref[base + slot, rows, :] = tile
            return store

        def run_round(blocks, k_src=k_src, v_src=v_src, store_for=store_for, sub_len=sub_len):
            state = [dict() for _ in blocks]

            def scores(b):
                r, n = blocks[b]
                row0 = _aligned(r * sub_len + n * BLOCK, BLOCK)
                qq = q2[:, pl.ds(row0, BLOCK), :].reshape(2 * BLOCK, LANES)
                if n > 0:
                    rows, bias = pl.ds(_aligned(row0 - BLOCK, BLOCK), 2 * BLOCK), bias_ref[...]
                else:
                    rows, bias = pl.ds(row0, BLOCK), bias_ref[:, BLOCK:]
                state[b].update(rows=rows, s=_dot_nt(qq, k_src[rows, :]) + bias)

            def row_max(b):
                state[b]["m"] = jnp.max(state[b]["s"], axis=-1, keepdims=True)

            def weighted_values(b):
                st = state[b]
                p = jnp.exp(st.pop("s") - st["m"]).astype(BF16)
                vb = v_src[st["rows"], :]
                st["pv"] = _dot(p, jnp.concatenate([vb, jnp.ones_like(vb)], axis=1))

            def store(b):
                st = state[b]
                pv = st.pop("pv")
                stats = (pv[:, :LANES], st.pop("m"), pv[:, LANES:])
                store_for(*blocks[b])([jnp.where(lo, t[:BLOCK], t[BLOCK:]) for t in stats])

            _skewed(len(blocks), (scores, row_max, weighted_values, store))

        subs_per_round = max(1, UNITS_PER_ROUND // n_blk)
        blks_per_round = min(n_blk, UNITS_PER_ROUND)
        for n0 in range(0, n_blk, blks_per_round):
            def rounds(i, carry, n0=n0, run_round=run_round):
                run_round([(i * subs_per_round + dr, n0 + dn)
                           for dr in range(subs_per_round) for dn in range(blks_per_round)])
                return carry

            if dil == subs_per_round:
                rounds(0, 0)
            else:
                lax.fori_loop(0, dil // subs_per_round, rounds, 0)

        if prev_dil > 1:
            for slot in range(3):
                for r_prev in range(prev_dil):
                    acc[3 * p_idx + slot, pl.ds(r_prev, prev_len, stride=prev_dil), :] = (
                        tmp[slot, r_prev * prev_len:(r_prev + 1) * prev_len, :])
        prev_dil = dil

    merge_rows = 256

    def merge(c, carry):
        rows = pl.ds(pl.multiple_of(c * merge_rows, merge_rows), merge_rows)
        maxes = [acc[3 * p + ROW_MAX, rows, :] for p in range(N_PAT)]
        top = functools.reduce(jnp.maximum, maxes)
        ws = [jnp.exp(mx - top) for mx in maxes]
        num = sum(w * acc[3 * p + PV, rows, :] for p, w in enumerate(ws))
        den = sum(w * acc[3 * p + ROW_SUM, rows, :] for p, w in enumerate(ws))
        o_ref[rows, :] = (num / den).astype(o_ref.dtype)
        return carry

    lax.fori_loop(0, seq // merge_rows, merge, 0)


def _dilated_attention(h):
    _, b, s, _ = h.shape
    blk = lambda group: pl.BlockSpec((HEAD_PAIRS, None, s, LANES), lambda i: (group, i, 0, 0))
    return pl.pallas_call(
        _dil_attn_kernel,
        grid=(b,),
        in_specs=[blk(0), blk(1), blk(2)],
        out_specs=blk(0),
        out_shape=jax.ShapeDtypeStruct((HEAD_PAIRS, b, s, LANES), BF16),
        scratch_shapes=[pltpu.VMEM((2 * 3, s, LANES), F32),
                        pltpu.VMEM((2, s, LANES), BF16),
                        pltpu.VMEM((s, LANES), BF16),
                        pltpu.VMEM((s, LANES), BF16),
                        pltpu.VMEM((2 * BLOCK, 2 * BLOCK), F32),
                        pltpu.VMEM((N_PAT * 3, s, LANES), F32),
                        pltpu.VMEM((3, s, LANES), F32)],
        compiler_params=_params("parallel"),
        name="dilated_attention",
    )(h, h, h)


MEM_PAIRS = MEM_WIDTH // LANES


def _mem_attn_kernel(q_ref, mkv_ref, o_ref):
    seq = q_ref.shape[1]
    rows_per_item = 256
    items_per_pair = seq // rows_per_item
    lo = lax.broadcasted_iota(jnp.int32, (rows_per_item, LANES), 1) < HEAD_DIM
    state = [dict() for _ in range(MEM_PAIRS * items_per_pair)]

    def place(b):
        pair, item = divmod(b, items_per_pair)
        return pair, slice(item * rows_per_item, (item + 1) * rows_per_item)

    def scores(b):
        pair, rows = place(b)
        qb = (q_ref[pair, rows, :].astype(F32) * Q_SCALE).astype(BF16)
        zero = jnp.zeros_like(qb)
        q2 = jnp.concatenate([jnp.where(lo, qb, zero), jnp.where(lo, zero, qb)], axis=0)
        state[b]["s"] = _dot_nt(q2, mkv_ref[:, pair * LANES:(pair + 1) * LANES])

    def row_max(b):
        state[b]["m"] = jnp.max(state[b]["s"], axis=-1, keepdims=True)

    def weighted_values(b):
        st = state[b]
        pair, _ = place(b)
        p = jnp.exp(st.pop("s") - st.pop("m")).astype(BF16)
        mv = mkv_ref[:, MEM_WIDTH + pair * LANES:MEM_WIDTH + (pair + 1) * LANES]
        st["pv"] = _dot(p, jnp.concatenate([mv, jnp.ones_like(mv)], axis=1))

    def store(b):
        pair, rows = place(b)
        pv = state[b].pop("pv")
        o = pv[:, :LANES] / pv[:, LANES:]
        o_ref[pair, rows, :] = jnp.where(lo, o[:rows_per_item], o[rows_per_item:]).astype(o_ref.dtype)

    _skewed(len(state), (scores, row_max, weighted_values, store))


def _memory_attention(q_tiles, first_tile, mkv):
    _, b, s, _ = q_tiles.shape
    assert first_tile % MEM_PAIRS == 0
    return pl.pallas_call(
        _mem_attn_kernel,
        grid=(b,),
        in_specs=[pl.BlockSpec((MEM_PAIRS, None, s, LANES), lambda i: (first_tile // MEM_PAIRS, i, 0, 0)),
                  pl.BlockSpec((None, N_MEM, 2 * MEM_WIDTH), lambda i: (i, 0, 0))],
        out_specs=pl.BlockSpec((MEM_PAIRS, None, s, LANES), lambda i: (0, i, 0, 0)),
        out_shape=jax.ShapeDtypeStruct((MEM_PAIRS, b, s, LANES), BF16),
        compiler_params=_params("parallel"),
        name="memory_attention",
    )(q_tiles, mkv)


GELU_C1 = (2.0 / 3.141592653589793) ** 0.5
GELU_C2 = GELU_C1 * 0.044715


def _gelu(x):
    return x * (0.5 * jnp.tanh(x * (GELU_C1 + GELU_C2 * (x * x))) + 0.5)


SGU_ROW_GROUPS = 2


def _proj_sgu_kernel(x_ref, w_ref, g_ref, b_ref, ws_ref, bias_ref, mix_ref, qm_ref, u_ref, vn_ref):
    tm = x_ref.shape[0]
    row = lax.broadcasted_iota(jnp.int32, (CHUNK, CHUNK), 0)
    col = lax.broadcasted_iota(jnp.int32, (CHUNK, CHUNK), 1)
    causal = row >= col
    lo = lax.broadcasted_iota(jnp.int32, (CHUNK, LANES), 1) < HEAD_DIM
    groups = [slice(r, r + tm // SGU_ROW_GROUPS) for r in range(0, tm, tm // SGU_ROW_GROUPS)]
    for rows in groups:
        h = _dot(x_ref[rows, :].astype(BF16), w_ref[...])
        u_ref[rows, :] = h[:, :SGU_WIDTH]
        vn_ref[rows, :] = _layer_norm(_gelu(h[:, SGU_WIDTH:2 * SGU_WIDTH]),
                                      g_ref[...], b_ref[...]).astype(BF16)
        for c in range(MEM_PAIRS):
            first = 2 * SGU_WIDTH + c * LANES
            qm_ref[c, rows, :] = h[:, first:first + LANES].astype(qm_ref.dtype)
    for gp in range(N_SGU_GROUPS // 2):
        cols = slice(gp * LANES, (gp + 1) * LANES)
        w0 = jnp.where(causal, ws_ref[2 * gp], 0.0).astype(BF16)
        w1 = jnp.where(causal, ws_ref[2 * gp + 1], 0.0).astype(BF16)
        bias = bias_ref[:, cols]
        for c in range(tm // CHUNK):
            rows = slice(c * CHUNK, (c + 1) * CHUNK)
            vc = vn_ref[rows, cols]
            mixed = jnp.where(lo, _dot(w0, vc), _dot(w1, vc)) + bias
            mix_ref[gp, rows, :] = (_gelu(u_ref[rows, cols]) * mixed).astype(mix_ref.dtype)


def _project_sgu(x, w, ln_g, ln_b, w_s, bias_full, tm):
    m, k = x.shape
    tiles = lambda n: pl.BlockSpec((n, tm, LANES), lambda i: (0, i, 0))
    return pl.pallas_call(
        _proj_sgu_kernel,
        grid=(m // tm,),
        in_specs=[pl.BlockSpec((tm, k), lambda i: (i, 0)), _resident(w.shape),
                  _resident((1, SGU_WIDTH)), _resident((1, SGU_WIDTH)),
                  _resident((N_SGU_GROUPS, CHUNK, CHUNK)), _resident((CHUNK, SGU_WIDTH))],
        out_specs=[tiles(SGU_WIDTH // LANES), tiles(MEM_PAIRS)],
        out_shape=[jax.ShapeDtypeStruct((SGU_WIDTH // LANES, m, LANES), BF16),
                   jax.ShapeDtypeStruct((MEM_PAIRS, m, LANES), BF16)],
        scratch_shapes=[pltpu.VMEM((tm, SGU_WIDTH), F32), pltpu.VMEM((tm, SGU_WIDTH), BF16)],
        compiler_params=_params("parallel"),
        name="in_proj_sgu",
    )(x, w, ln_g, ln_b, w_s, bias_full)


def _tail_kernel(mix_ref, mo_ref, x_ref, wo_ref, g1_ref, b1_ref, wg_ref, wu_ref, wd_ref,
                 g2_ref, b2_ref, o_ref, x1_ref, xb_ref, hid_ref):
    tm = x_ref.shape[0]
    groups = [slice(r, r + tm // TAIL_ROW_GROUPS) for r in range(0, tm, tm // TAIL_ROW_GROUPS)]
    for rows in groups:
        mixed = jnp.concatenate([ref[c, rows, :] for ref in (mix_ref, mo_ref) for c in range(ref.shape[0])],
                                axis=1)
        x1 = _layer_norm(DN_ALPHA * x_ref[rows, :] + _dot(mixed, wo_ref[...]), g1_ref[...], b1_ref[...])
        x1_ref[rows, :] = x1
        xb_ref[rows, :] = x1.astype(BF16)
    for c in range(wg_ref.shape[1] // FF_CHUNK):
        cols = slice(c * FF_CHUNK, (c + 1) * FF_CHUNK)
        for rows in groups:
            xb = xb_ref[rows, :]
            hid = jax.nn.silu(_dot(xb, wg_ref[:, cols])) * _dot(xb, wu_ref[:, cols])
            hid_ref[rows, cols] = hid.astype(BF16)
    for rows in groups:
        f = _dot(hid_ref[rows, :], wd_ref[...])
        o_ref[rows, :] = _layer_norm(DN_ALPHA * x1_ref[rows, :] + f, g2_ref[...], b2_ref[...])


def _layer_tail(mix, mo, x, w_out, g1, b1, wg, wu, wd, g2, b2, tm):
    m, d = x.shape
    f = wg.shape[1]
    tiles = lambda a: pl.BlockSpec((a.shape[0], tm, LANES), lambda i: (0, i, 0))
    rows = pl.BlockSpec((tm, d), lambda i: (i, 0))
    return pl.pallas_call(
        _tail_kernel,
        grid=(m // tm,),
        in_specs=[tiles(mix), tiles(mo), rows,
                  _resident((d, d)), _resident((1, d)), _resident((1, d)),
                  _resident((d, f)), _resident((d, f)), _resident((f, d)),
                  _resident((1, d)), _resident((1, d))],
        out_specs=rows,
        out_shape=jax.ShapeDtypeStruct((m, d), F32),
        scratch_shapes=[pltpu.VMEM((tm, d), F32), pltpu.VMEM((tm, d), BF16), pltpu.VMEM((tm, f), BF16)],
        compiler_params=_params("parallel"),
        name="layer_tail",
    )(mix, mo, x, w_out, g1, b1, wg, wu, wd, g2, b2)


def kernel(x, mem, a_w_in, b_w_in, sgu_ln_g, sgu_ln_b, sgu_w_s, sgu_b_s, w_mem_kv, w_out,
           ln_mix_g, ln_mix_b, w_gate, w_up, w_down, ln_ffn_g, ln_ffn_b):
    bsz, seq, d = x.shape
    m = bsz * seq
    xf = x.reshape(m, d)
    memf = mem.reshape(bsz * N_MEM, d)
    row = lambda a: a.reshape(1, -1)
    for i in range(DEPTH):
        j = i // 2
        (mkv,) = _project(memf, w_mem_kv[i].astype(BF16), [(0, 2 * MEM_WIDTH, False)], [BF16], tm=512)
        mkv = mkv.reshape(bsz, N_MEM, 2 * MEM_WIDTH)
        per_batch = lambda t: t.reshape(t.shape[0], bsz, seq, LANES)
        if i % 2 == 0:
            (h,) = _project(xf, a_w_in[j].astype(BF16), [(0, 3 * DIL_WIDTH + MEM_WIDTH, True)], [BF16],
                            tm=1024)
            h = per_batch(h)
            mix = _dilated_attention(h).reshape(HEAD_PAIRS, m, LANES)
            mo = _memory_attention(h, 3 * HEAD_PAIRS, mkv)
        else:
            bias_full = jnp.repeat(sgu_b_s[j].T, HEAD_DIM, axis=1)
            mix, qm = _project_sgu(xf, b_w_in[j].astype(BF16), row(sgu_ln_g[j]), row(sgu_ln_b[j]),
                                   sgu_w_s[j], bias_full, tm=512)
            mo = _memory_attention(per_batch(qm), 0, mkv)
        xf = _layer_tail(mix, mo.reshape(MEM_PAIRS, m, LANES), xf, w_out[i].astype(BF16),
                         row(ln_mix_g[i]), row(ln_mix_b[i]),
                         w_gate[i].astype(BF16), w_up[i].astype(BF16), w_down[i].astype(BF16),
                         row(ln_ffn_g[i]), row(ln_ffn_b[i]), tm=512)
    return xf.reshape(bsz, seq, d)
```

```python
import functools

import jax
import jax.numpy as jnp
from jax import lax
from jax.experimental import pallas as pl
from jax.experimental.pallas import tpu as pltpu

F32 = jnp.float32
BF16 = jnp.bfloat16

D_MODEL = 1024
N_MEM = 256
HEAD_DIM = 64
N_DIL_HEADS = 12
DIL_WIDTH = N_DIL_HEADS * HEAD_DIM
DIL_PATTERNS = ((128, 1), (512, 4), (2048, 16))
BLOCK = 128
N_SGU_GROUPS = 12
SGU_WIDTH = N_SGU_GROUPS * HEAD_DIM
CHUNK = 128
MEM_WIDTH = 4 * HEAD_DIM
DEPTH = 4
DN_ALPHA = (2 * DEPTH) ** 0.25
LN_EPS = 1e-5

LANES = 128
HEAD_PAIRS = DIL_WIDTH // LANES
Q_SCALE = HEAD_DIM ** -0.5
VMEM_LIMIT = 56 * 1024 * 1024
FF_CHUNK = 256
TAIL_ROW_GROUPS = 2
STAGE_LAGS = (0, 3, 4, 7)


def _params(*sem):
    return pltpu.CompilerParams(dimension_semantics=sem, vmem_limit_bytes=VMEM_LIMIT)


def _layer_norm(z, g, b):
    mu = jnp.mean(z, axis=-1, keepdims=True)
    zc = z - mu
    var = jnp.mean(zc * zc, axis=-1, keepdims=True)
    return zc * lax.rsqrt(var + LN_EPS) * g + b


def _aligned(idx, multiple):
    return idx if isinstance(idx, int) else pl.multiple_of(idx, multiple)


def _dot(a, b):
    return jnp.dot(a, b, preferred_element_type=F32)


def _dot_nt(a, b):
    return lax.dot_general(a, b, (((1,), (1,)), ((), ())), preferred_element_type=F32)


def _resident(shape):
    return pl.BlockSpec(shape, lambda *_: (0,) * len(shape), pipeline_mode=pl.Buffered(1))


def _skewed(n_items, stages):
    for t in range(n_items + STAGE_LAGS[-1]):
        for stage, lag in zip(stages, STAGE_LAGS):
            if 0 <= t - lag < n_items:
                stage(t - lag)


def _proj_kernel(splits, x_ref, w_ref, *out_refs):
    acc = _dot(x_ref[...].astype(BF16), w_ref[...])
    for (start, width, tiled), o_ref in zip(splits, out_refs):
        if tiled:
            for c in range(width // LANES):
                o_ref[c] = acc[:, start + c * LANES:start + (c + 1) * LANES].astype(o_ref.dtype)
        else:
            o_ref[...] = acc[:, start:start + width].astype(o_ref.dtype)


def _project(x, w, splits, dtypes, tm):
    m, k = x.shape
    n = w.shape[1]
    out_specs, out_shape = [], []
    for (_, width, tiled), dt in zip(splits, dtypes):
        if tiled:
            out_specs.append(pl.BlockSpec((width // LANES, tm, LANES), lambda i: (0, i, 0)))
            out_shape.append(jax.ShapeDtypeStruct((width // LANES, m, LANES), dt))
        else:
            out_specs.append(pl.BlockSpec((tm, width), lambda i: (i, 0)))
            out_shape.append(jax.ShapeDtypeStruct((m, width), dt))
    return pl.pallas_call(
        functools.partial(_proj_kernel, splits),
        grid=(m // tm,),
        in_specs=[pl.BlockSpec((tm, k), lambda i: (i, 0)), _resident((k, n))],
        out_specs=out_specs,
        out_shape=out_shape,
        compiler_params=_params("parallel"),
        name="in_proj",
    )(x, w)


N_PAT = len(DIL_PATTERNS)
Q, K, V = range(3)
PV, ROW_MAX, ROW_SUM = range(3)
UNITS_PER_ROUND = 16


def _dil_attn_kernel(q_ref, k_ref, v_ref, o_ref, *scratch):
    def pair_body(pair, carry):
        _attend_pair(pair, q_ref.at[pair], k_ref.at[pair], v_ref.at[pair], o_ref.at[pair], *scratch)
        return carry

    lax.fori_loop(0, q_ref.shape[0], pair_body, 0)


def _attend_pair(hp, q_ref, k_ref, v_ref, o_ref, lay, q2, kp, vp, bias_ref, acc, tmp):
    seq = q_ref.shape[0]
    lo = lax.broadcasted_iota(jnp.int32, (BLOCK, LANES), 1) < HEAD_DIM

    lay[Q] = q_ref[...].astype(F32) * Q_SCALE
    lay[K] = k_ref[...].astype(F32)
    lay[V] = v_ref[...].astype(F32)

    row = lax.broadcasted_iota(jnp.int32, (2 * BLOCK, 2 * BLOCK), 0)
    kc = lax.broadcasted_iota(jnp.int32, (2 * BLOCK, 2 * BLOCK), 1)
    steps = (row & (BLOCK - 1)) + BLOCK - kc
    head = (jnp.full((2 * BLOCK, 2 * BLOCK), 2 * hp, jnp.int32) + (row >> 7)).astype(F32)
    slope = jnp.exp2(-8.0 * (head + 1.0) / N_DIL_HEADS)

    prev_dil = 1
    for p_idx, (window, dil) in enumerate(DIL_PATTERNS):
        sub_len = seq // dil
        n_blk = sub_len // BLOCK
        ratio = dil // prev_dil
        prev_len = seq // prev_dil
        src, dst = 3 * ((p_idx + 1) % 2), 3 * (p_idx % 2)
        assert prev_dil * ratio == dil and (prev_dil == 1 or p_idx == 2)

        valid = (steps >= 0) & (steps <= window // dil)
        bias_ref[...] = jnp.where(valid, -slope * (steps * dil).astype(F32), -jnp.inf)

        if ratio == 1:
            lo_all = lax.broadcasted_iota(jnp.int32, (seq, LANES), 1) < HEAD_DIM
            q2[0] = jnp.where(lo_all, lay[dst + Q], 0.0).astype(BF16)
            q2[1] = jnp.where(lo_all, 0.0, lay[dst + Q]).astype(BF16)
            k_src, v_src = k_ref, v_ref
        else:
            lo_sub = lax.broadcasted_iota(jnp.int32, (sub_len, LANES), 1) < HEAD_DIM
            for r_prev in range(prev_dil):
                for c in range(ratio):
                    rows = pl.ds(r_prev * prev_len + c, sub_len, stride=ratio)
                    out_rows = pl.ds((r_prev + prev_dil * c) * sub_len, sub_len)
                    qr, kr, vr = lay[src + Q, rows, :], lay[src + K, rows, :], lay[src + V, rows, :]
                    if p_idx + 1 < N_PAT:
                        lay[dst + Q, out_rows, :] = qr
                        lay[dst + K, out_rows, :] = kr
                        lay[dst + V, out_rows, :] = vr
                    q2[0, out_rows, :] = jnp.where(lo_sub, qr, 0.0).astype(BF16)
                    q2[1, out_rows, :] = jnp.where(lo_sub, 0.0, qr).astype(BF16)
                    kp[out_rows, :] = kr.astype(BF16)
                    vp[out_rows, :] = vr.astype(BF16)
            k_src, v_src = kp, vp

        def store_for(r, n, p_idx=p_idx, dil=dil, ratio=ratio, prev_dil=prev_dil, prev_len=prev_len):
            if dil == 1:
                ref, base, rows = acc, 3 * p_idx, pl.ds(_aligned(n * BLOCK, BLOCK), BLOCK)
            elif prev_dil == 1:
                ref, base, rows = acc, 3 * p_idx, pl.ds(n * BLOCK * dil + r, BLOCK, stride=dil)
            else:
                start = lax.rem(r, prev_dil) * prev_len + n * BLOCK * ratio + lax.div(r, prev_dil)
                ref, base, rows = tmp, 0, pl.ds(start, BLOCK, stride=ratio)

            def store(tiles):
                for slot, tile in enumerate(tiles):
                    ref[base + slot, rows, :] = tile
            return store

        def run_round(blocks, k_src=k_src, v_src=v_src, store_for=store_for, sub_len=sub_len):
            state = [dict() for _ in blocks]

            def scores(b):
                r, n = blocks[b]
                row0 = _aligned(r * sub_len + n * BLOCK, BLOCK)
                qq = q2[:, pl.ds(row0, BLOCK), :].reshape(2 * BLOCK, LANES)
                if n > 0:
                    rows, bias = pl.ds(_aligned(row0 - BLOCK, BLOCK), 2 * BLOCK), bias_ref[...]
                else:
                    rows, bias = pl.ds(row0, BLOCK), bias_ref[:, BLOCK:]
                state[b].update(rows=rows, s=_dot_nt(qq, k_src[rows, :]) + bias)

            def row_max(b):
                state[b]["m"] = jnp.max(state[b]["s"], axis=-1, keepdims=True)

            def weighted_values(b):
                st = state[b]
                p = jnp.exp(st.pop("s") - st["m"]).astype(BF16)
                vb = v_src[st["rows"], :]
                st["pv"] = _dot(p, jnp.concatenate([vb, jnp.ones_like(vb)], axis=1))

            def store(b):
                st = state[b]
                pv = st.pop("pv")
                stats = (pv[:, :LANES], st.pop("m"), pv[:, LANES:])
                store_for(*blocks[b])([jnp.where(lo, t[:BLOCK], t[BLOCK:]) for t in stats])

            _skewed(len(blocks), (scores, row_max, weighted_values, store))

        subs_per_round = max(1, UNITS_PER_ROUND // n_blk)
        blks_per_round = min(n_blk, UNITS_PER_ROUND)
        for n0 in range(0, n_blk, blks_per_round):
            def rounds(i, carry, n0=n0, run_round=run_round):
                run_round([(i * subs_per_round + dr, n0 + dn)
                           for dr in range(subs_per_round) for dn in range(blks_per_round)])
                return carry

            if dil == subs_per_round:
                rounds(0, 0)
            else:
                lax.fori_loop(0, dil // subs_per_round, rounds, 0)

        if prev_dil > 1:
            for slot in range(3):
                for r_prev in range(prev_dil):
                    acc[3 * p_idx + slot, pl.ds(r_prev, prev_len, stride=prev_dil), :] = (
                        tmp[slot, r_prev * prev_len:(r_prev + 1) * prev_len, :])
        prev_dil = dil

    merge_rows = 256

    def merge(c, carry):
        rows = pl.ds(pl.multiple_of(c * merge_rows, merge_rows), merge_rows)
        maxes = [acc[3 * p + ROW_MAX, rows, :] for p in range(N_PAT)]
        top = functools.reduce(jnp.maximum, maxes)
        ws = [jnp.exp(mx - top) for mx in maxes]
        num = sum(w * acc[3 * p + PV, rows, :] for p, w in enumerate(ws))
        den = sum(w * acc[3 * p + ROW_SUM, rows, :] for p, w in enumerate(ws))
        o_ref[rows, :] = (num / den).astype(o_ref.dtype)
        return carry

    lax.fori_loop(0, seq // merge_rows, merge, 0)


def _dilated_attention(h):
    _, b, s, _ = h.shape
    blk = lambda group: pl.BlockSpec((HEAD_PAIRS, None, s, LANES), lambda i: (group, i, 0, 0))
    return pl.pallas_call(
        _dil_attn_kernel,
        grid=(b,),
        in_specs=[blk(0), blk(1), blk(2)],
        out_specs=blk(0),
        out_shape=jax.ShapeDtypeStruct((HEAD_PAIRS, b, s, LANES), BF16),
        scratch_shapes=[pltpu.VMEM((2 * 3, s, LANES), F32),
                        pltpu.VMEM((2, s, LANES), BF16),
                        pltpu.VMEM((s, LANES), BF16),
                        pltpu.VMEM((s, LANES), BF16),
                        pltpu.VMEM((2 * BLOCK, 2 * BLOCK), F32),
                        pltpu.VMEM((N_PAT * 3, s, LANES), F32),
                        pltpu.VMEM((3, s, LANES), F32)],
        compiler_params=_params("parallel"),
        name="dilated_attention",
    )(h, h, h)


MEM_PAIRS = MEM_WIDTH // LANES


def _mem_attn_kernel(q_ref, mkv_ref, o_ref):
    seq = q_ref.shape[1]
    rows_per_item = 256
    items_per_pair = seq // rows_per_item
    lo = lax.broadcasted_iota(jnp.int32, (rows_per_item, LANES), 1) < HEAD_DIM
    state = [dict() for _ in range(MEM_PAIRS * items_per_pair)]

    def place(b):
        pair, item = divmod(b, items_per_pair)
        return pair, slice(item * rows_per_item, (item + 1) * rows_per_item)

    def scores(b):
        pair, rows = place(b)
        qb = (q_ref[pair, rows, :].astype(F32) * Q_SCALE).astype(BF16)
        zero = jnp.zeros_like(qb)
        q2 = jnp.concatenate([jnp.where(lo, qb, zero), jnp.where(lo, zero, qb)], axis=0)
        state[b]["s"] = _dot_nt(q2, mkv_ref[:, pair * LANES:(pair + 1) * LANES])

    def row_max(b):
        state[b]["m"] = jnp.max(state[b]["s"], axis=-1, keepdims=True)

    def weighted_values(b):
        st = state[b]
        pair, _ = place(b)
        p = jnp.exp(st.pop("s") - st.pop("m")).astype(BF16)
        mv = mkv_ref[:, MEM_WIDTH + pair * LANES:MEM_WIDTH + (pair + 1) * LANES]
        st["pv"] = _dot(p, jnp.concatenate([mv, jnp.ones_like(mv)], axis=1))

    def store(b):
        pair, rows = place(b)
        pv = state[b].pop("pv")
        o = pv[:, :LANES] / pv[:, LANES:]
        o_ref[pair, rows, :] = jnp.where(lo, o[:rows_per_item], o[rows_per_item:]).astype(o_ref.dtype)

    _skewed(len(state), (scores, row_max, weighted_values, store))


def _memory_attention(q_tiles, first_tile, mkv):
    _, b, s, _ = q_tiles.shape
    assert first_tile % MEM_PAIRS == 0
    return pl.pallas_call(
        _mem_attn_kernel,
        grid=(b,),
        in_specs=[pl.BlockSpec((MEM_PAIRS, None, s, LANES), lambda i: (first_tile // MEM_PAIRS, i, 0, 0)),
                  pl.BlockSpec((None, N_MEM, 2 * MEM_WIDTH), lambda i: (i, 0, 0))],
        out_specs=pl.BlockSpec((MEM_PAIRS, None, s, LANES), lambda i: (0, i, 0, 0)),
        out_shape=jax.ShapeDtypeStruct((MEM_PAIRS, b, s, LANES), BF16),
        compiler_params=_params("parallel"),
        name="memory_attention",
    )(q_tiles, mkv)


GELU_C1 = (2.0 / 3.141592653589793) ** 0.5
GELU_C2 = GELU_C1 * 0.044715


def _gelu(x):
    return x * (0.5 * jnp.tanh(x * (GELU_C1 + GELU_C2 * (x * x))) + 0.5)


SGU_ROW_GROUPS = 2


def _proj_sgu_kernel(x_ref, w_ref, g_ref, b_ref, ws_ref, bias_ref, mix_ref, qm_ref, u_ref, vn_ref):
    tm = x_ref.shape[0]
    row = lax.broadcasted_iota(jnp.int32, (CHUNK, CHUNK), 0)
    col = lax.broadcasted_iota(jnp.int32, (CHUNK, CHUNK), 1)
    causal = row >= col
    lo = lax.broadcasted_iota(jnp.int32, (CHUNK, LANES), 1) < HEAD_DIM
    groups = [slice(r, r + tm // SGU_ROW_GROUPS) for r in range(0, tm, tm // SGU_ROW_GROUPS)]
    for rows in groups:
        h = _dot(x_ref[rows, :].astype(BF16), w_ref[...])
        u_ref[rows, :] = h[:, :SGU_WIDTH]
        vn_ref[rows, :] = _layer_norm(_gelu(h[:, SGU_WIDTH:2 * SGU_WIDTH]),
                                      g_ref[...], b_ref[...]).astype(BF16)
        for c in range(MEM_PAIRS):
            first = 2 * SGU_WIDTH + c * LANES
            qm_ref[c, rows, :] = h[:, first:first + LANES].astype(qm_ref.dtype)
    for gp in range(N_SGU_GROUPS // 2):
        cols = slice(gp * LANES, (gp + 1) * LANES)
        w0 = jnp.where(causal, ws_ref[2 * gp], 0.0).astype(BF16)
        w1 = jnp.where(causal, ws_ref[2 * gp + 1], 0.0).astype(BF16)
        bias = bias_ref[:, cols]
        for c in range(tm // CHUNK):
            rows = slice(c * CHUNK, (c + 1) * CHUNK)
            vc = vn_ref[rows, cols]
            mixed = jnp.where(lo, _dot(w0, vc), _dot(w1, vc)) + bias
            mix_ref[gp, rows, :] = (_gelu(u_ref[rows, cols]) * mixed).astype(mix_ref.dtype)


def _project_sgu(x, w, ln_g, ln_b, w_s, bias_full, tm):
    m, k = x.shape
    tiles = lambda n: pl.BlockSpec((n, tm, LANES), lambda i: (0, i, 0))
    return pl.pallas_call(
        _proj_sgu_kernel,
        grid=(m // tm,),
        in_specs=[pl.BlockSpec((tm, k), lambda i: (i, 0)), _resident(w.shape),
                  _resident((1, SGU_WIDTH)), _resident((1, SGU_WIDTH)),
                  _resident((N_SGU_GROUPS, CHUNK, CHUNK)), _resident((CHUNK, SGU_WIDTH))],
        out_specs=[tiles(SGU_WIDTH // LANES), tiles(MEM_PAIRS)],
        out_shape=[jax.ShapeDtypeStruct((SGU_WIDTH // LANES, m, LANES), BF16),
                   jax.ShapeDtypeStruct((MEM_PAIRS, m, LANES), BF16)],
        scratch_shapes=[pltpu.VMEM((tm, SGU_WIDTH), F32), pltpu.VMEM((tm, SGU_WIDTH), BF16)],
        compiler_params=_params("parallel"),
        name="in_proj_sgu",
    )(x, w, ln_g, ln_b, w_s, bias_full)


def _tail_kernel(mix_ref, mo_ref, x_ref, wo_ref, g1_ref, b1_ref, wg_ref, wu_ref, wd_ref,
                 g2_ref, b2_ref, o_ref, x1_ref, xb_ref, hid_ref):
    tm = x_ref.shape[0]
    groups = [slice(r, r + tm // TAIL_ROW_GROUPS) for r in range(0, tm, tm // TAIL_ROW_GROUPS)]
    for rows in groups:
        y, first = 0.0, 0
        for ref in (mix_ref, mo_ref):
            width = ref.shape[0] * LANES
            lhs = jnp.concatenate([ref[c, rows, :] for c in range(ref.shape[0])], axis=1)
            y = y + _dot(lhs, wo_ref[first:first + width, :])
            first += width
        x1 = _layer_norm(DN_ALPHA * x_ref[rows, :] + y, g1_ref[...], b1_ref[...])
        x1_ref[rows, :] = x1
        xb_ref[rows, :] = x1.astype(BF16)
    for c in range(wg_ref.shape[1] // FF_CHUNK):
        cols = slice(c * FF_CHUNK, (c + 1) * FF_CHUNK)
        for rows in groups:
            xb = xb_ref[rows, :]
            hid = jax.nn.silu(_dot(xb, wg_ref[:, cols])) * _dot(xb, wu_ref[:, cols])
            hid_ref[rows, cols] = hid.astype(BF16)
    for rows in groups:
        f = _dot(hid_ref[rows, :], wd_ref[...])
        o_ref[rows, :] = _layer_norm(DN_ALPHA * x1_ref[rows, :] + f, g2_ref[...], b2_ref[...])


def _layer_tail(mix, mo, x, w_out, g1, b1, wg, wu, wd, g2, b2, tm):
    m, d = x.shape
    f = wg.shape[1]
    tiles = lambda a: pl.BlockSpec((a.shape[0], tm, LANES), lambda i: (0, i, 0))
    rows = pl.BlockSpec((tm, d), lambda i: (i, 0))
    return pl.pallas_call(
        _tail_kernel,
        grid=(m // tm,),
        in_specs=[tiles(mix), tiles(mo), rows,
                  _resident((d, d)), _resident((1, d)), _resident((1, d)),
                  _resident((d, f)), _resident((d, f)), _resident((f, d)),
                  _resident((1, d)), _resident((1, d))],
        out_specs=rows,
        out_shape=jax.ShapeDtypeStruct((m, d), F32),
        scratch_shapes=[pltpu.VMEM((tm, d), F32), pltpu.VMEM((tm, d), BF16), pltpu.VMEM((tm, f), BF16)],
        compiler_params=_params("parallel"),
        name="layer_tail",
    )(mix, mo, x, w_out, g1, b1, wg, wu, wd, g2, b2)


def kernel(x, mem, a_w_in, b_w_in, sgu_ln_g, sgu_ln_b, sgu_w_s, sgu_b_s, w_mem_kv, w_out,
           ln_mix_g, ln_mix_b, w_gate, w_up, w_down, ln_ffn_g, ln_ffn_b):
    bsz, seq, d = x.shape
    m = bsz * seq
    xf = x.reshape(m, d)
    memf = mem.reshape(bsz * N_MEM, d)
    row = lambda a: a.reshape(1, -1)
    for i in range(DEPTH):
        j = i // 2
        (mkv,) = _project(memf, w_mem_kv[i].astype(BF16), [(0, 2 * MEM_WIDTH, False)], [BF16], tm=512)
        mkv = mkv.reshape(bsz, N_MEM, 2 * MEM_WIDTH)
        per_batch = lambda t: t.reshape(t.shape[0], bsz, seq, LANES)
        if i % 2 == 0:
            (h,) = _project(xf, a_w_in[j].astype(BF16), [(0, 3 * DIL_WIDTH + MEM_WIDTH, True)], [BF16],
                            tm=1024)
            h = per_batch(h)
            mix = _dilated_attention(h).reshape(HEAD_PAIRS, m, LANES)
            mo = _memory_attention(h, 3 * HEAD_PAIRS, mkv)
        else:
            bias_full = jnp.repeat(sgu_b_s[j].T, HEAD_DIM, axis=1)
            mix, qm = _project_sgu(xf, b_w_in[j].astype(BF16), row(sgu_ln_g[j]), row(sgu_ln_b[j]),
                                   sgu_w_s[j], bias_full, tm=512)
            mo = _memory_attention(per_batch(qm), 0, mkv)
        xf = _layer_tail(mix, mo.reshape(MEM_PAIRS, m, LANES), xf, w_out[i].astype(BF16),
                         row(ln_mix_g[i]), row(ln_mix_b[i]),
                         w_gate[i].astype(BF16), w_up[i].astype(BF16), w_down[i].astype(BF16),
                         row(ln_ffn_g[i]), row(ln_ffn_b[i]), tm=512)
    return xf.reshape(bsz, seq, d)
```
